```python
import jax, jax.numpy as jnp
from jax import lax
import numpy as np

D_MODEL = 1024
BATCH = 8
SEQ = 8192
DEPTH = 4

GRID_W = 64
CTX_LEN = 256
N_MIXERS = 3
N_A_LAYERS = (DEPTH + 2) // 3
N_B_LAYERS = (DEPTH + 1) // 3
N_C_LAYERS = DEPTH // 3

D_FF = 4 * D_MODEL
N_MOD = 6
EPS = 1e-6
ROPE_BASE = 10000.0
Q_BLOCK = 128
NEG_INF = -1e30

MLA_HEADS = 16
MLA_Q_RANK = 256
MLA_KV_RANK = 128
MLA_NOPE = 64
MLA_ROPE = 32
MLA_V = 64

SWA_Q_HEADS = 16
SWA_KV_HEADS = 4
SWA_HEAD_DIM = 64
SWA_WINDOW = 128
SWA_BLOCK = 128

NA_HEADS = 16
NA_HEAD_DIM = 64
NA_ROWS = 8
NA_COLS = 16

kernel_name = "hybrid_interleaved_mla_swa_na_dit"


def rmsnorm(x, g):
    x32 = x.astype(jnp.float32)
    y = x32 * lax.rsqrt(jnp.mean(x32 * x32, axis=-1, keepdims=True) + EPS)
    return (y * g.astype(jnp.float32)).astype(x.dtype)


def modulate(h, shift, scale):
    return h * (1 + scale) + shift


def axial_rope_angles(n_tokens, rot_dim):
    t = jnp.arange(n_tokens, dtype=jnp.int32)
    row = (t // GRID_W).astype(jnp.float32)
    col = (t % GRID_W).astype(jnp.float32)
    n_freq = rot_dim // 4
    inv_freq = ROPE_BASE ** (-jnp.arange(n_freq, dtype=jnp.float32) / n_freq)
    ang = jnp.concatenate([row[:, None] * inv_freq, col[:, None] * inv_freq], axis=-1)
    return jnp.cos(ang), jnp.sin(ang)


def apply_rope(x, cos, sin):
    x32 = x.astype(jnp.float32).reshape(x.shape[:-1] + (x.shape[-1] // 2, 2))
    x1, x2 = x32[..., 0], x32[..., 1]
    c = cos[None, :, None, :]
    s = sin[None, :, None, :]
    out = jnp.stack([x1 * c - x2 * s, x1 * s + x2 * c], axis=-1)
    return out.reshape(x.shape).astype(x.dtype)


def ctx_attention(q, k, v, scale):
    s = jnp.einsum('bqhd,bkhd->bhqk', q, k, preferred_element_type=jnp.float32) * scale
    p = jax.nn.softmax(s, axis=-1).astype(v.dtype)
    return jnp.einsum('bhqk,bkhd->bqhd', p, v)


def mla_queries(q_lat, qa_g, w_qb, rope):
    b, n, _ = q_lat.shape
    q = (rmsnorm(q_lat, qa_g) @ w_qb).reshape(b, n, MLA_HEADS, MLA_NOPE + MLA_ROPE)
    q_nope, q_pe = q[..., :MLA_NOPE], q[..., MLA_NOPE:]
    if rope is not None:
        q_pe = apply_rope(q_pe, rope[0], rope[1])
    return jnp.concatenate([q_nope, q_pe], axis=-1)


def mla_keys_values(kv_lat, k_pe, kva_g, w_kvb, rope):
    b, n, _ = kv_lat.shape
    kv = (rmsnorm(kv_lat, kva_g) @ w_kvb).reshape(b, n, MLA_HEADS, MLA_NOPE + MLA_V)
    k_nope, v = kv[..., :MLA_NOPE], kv[..., MLA_NOPE:]
    k_pe = k_pe[:, :, None, :]
    if rope is not None:
        k_pe = apply_rope(k_pe, rope[0], rope[1])
    k = jnp.concatenate([k_nope, jnp.broadcast_to(k_pe, k_nope.shape[:-1] + (MLA_ROPE,))], axis=-1)
    return k, v


def mixer_mla(h, hc, w_in, qa_g, w_qb, kva_g, w_kvb, w_o, need_ctx):
    b, n, _ = h.shape
    L = hc.shape[1]
    scale = (MLA_NOPE + MLA_ROPE) ** -0.5
    rope = axial_rope_angles(n, MLA_ROPE)
    q_lat, kv_lat, k_pe = jnp.split(h @ w_in, [MLA_Q_RANK, MLA_Q_RANK + MLA_KV_RANK], axis=-1)
    q = mla_queries(q_lat, qa_g, w_qb, rope)
    k, v = mla_keys_values(kv_lat, k_pe, kva_g, w_kvb, rope)
    if need_ctx:
        qc_lat, kvc_lat, kc_pe = jnp.split(hc @ w_in, [MLA_Q_RANK, MLA_Q_RANK + MLA_KV_RANK], axis=-1)
    else:
        kvc_lat, kc_pe = jnp.split(hc @ w_in[:, MLA_Q_RANK:], [MLA_KV_RANK], axis=-1)
    kc, vc = mla_keys_values(kvc_lat, kc_pe, kva_g, w_kvb, None)
    k_all = jnp.concatenate([kc, k], axis=1)
    v_all = jnp.concatenate([vc, v], axis=1)
    nb = n // Q_BLOCK
    qb = q.reshape(b, nb, Q_BLOCK, MLA_HEADS, MLA_NOPE + MLA_ROPE).transpose(1, 0, 2, 3, 4)

    def block(qi):
        s = jnp.einsum('bqhd,bkhd->bhqk', qi, k_all, preferred_element_type=jnp.float32) * scale
        p = jax.nn.softmax(s, axis=-1).astype(v_all.dtype)
        return jnp.einsum('bhqk,bkhd->bqhd', p, v_all)

    o = lax.map(block, qb)
    y = o.transpose(1, 0, 2, 3, 4).reshape(b, n, MLA_HEADS * MLA_V) @ w_o
    yc = None
    if need_ctx:
        qc = mla_queries(qc_lat, qa_g, w_qb, None)
        yc = ctx_attention(qc, kc, vc, scale).reshape(b, L, MLA_HEADS * MLA_V) @ w_o
    return y, yc


def mixer_swa(h, hc, w_qkv, sink, w_o, need_ctx):
    b, n, _ = h.shape
    L = hc.shape[1]
    grp = SWA_Q_HEADS // SWA_KV_HEADS
    dq = SWA_Q_HEADS * SWA_HEAD_DIM
    dkv = SWA_KV_HEADS * SWA_HEAD_DIM
    scale = SWA_HEAD_DIM ** -0.5
    cos, sin = axial_rope_angles(n, SWA_HEAD_DIM)
    q, k, v = jnp.split(h @ w_qkv, [dq, dq + dkv], axis=-1)
    q = apply_rope(q.reshape(b, n, SWA_Q_HEADS, SWA_HEAD_DIM), cos, sin)
    q = q.reshape(b, n, SWA_KV_HEADS, grp, SWA_HEAD_DIM)
    k = apply_rope(k.reshape(b, n, SWA_KV_HEADS, SWA_HEAD_DIM), cos, sin)
    v = v.reshape(b, n, SWA_KV_HEADS, SWA_HEAD_DIM)
    kc, vc = jnp.split(hc @ w_qkv[:, dq:], [dkv], axis=-1)
    kc = kc.reshape(b, L, SWA_KV_HEADS, SWA_HEAD_DIM)
    vc = vc.reshape(b, L, SWA_KV_HEADS, SWA_HEAD_DIM)
    sink_g = sink.astype(jnp.float32).reshape(SWA_KV_HEADS, grp)
    pad = ((0, 0), (SWA_BLOCK, SWA_BLOCK), (0, 0), (0, 0))
    kp = jnp.pad(k, pad)
    vp = jnp.pad(v, pad)
    n_band = 3 * SWA_BLOCK

    def block(bi):
        start = bi * SWA_BLOCK
        qi = lax.dynamic_slice_in_dim(q, start, SWA_BLOCK, axis=1)
        ki = lax.dynamic_slice_in_dim(kp, start, n_band, axis=1)
        vi = lax.dynamic_slice_in_dim(vp, start, n_band, axis=1)
        qpos = start + jnp.arange(SWA_BLOCK)
        kpos = start - SWA_BLOCK + jnp.arange(n_band)
        valid = (kpos >= 0)[None, :] & (kpos < n)[None, :] & (jnp.abs(qpos[:, None] - kpos[None, :]) <= SWA_WINDOW)
        s_loc = jnp.einsum('bqhgd,bkhd->bhgqk', qi, ki, preferred_element_type=jnp.float32) * scale
        s_loc = jnp.where(valid, s_loc, NEG_INF)
        s_ctx = jnp.einsum('bqhgd,bkhd->bhgqk', qi, kc, preferred_element_type=jnp.float32) * scale
        s_sink = jnp.broadcast_to(sink_g[None, :, :, None, None], s_loc.shape[:-1] + (1,))
        p = jax.nn.softmax(jnp.concatenate([s_loc, s_ctx, s_sink], axis=-1), axis=-1)
        p_loc = p[..., :n_band].astype(v.dtype)
        p_ctx = p[..., n_band:n_band + L].astype(v.dtype)
        return (jnp.einsum('bhgqk,bkhd->bqhgd', p_loc, vi)
                + jnp.einsum('bhgqk,bkhd->bqhgd', p_ctx, vc))

    o = lax.map(block, jnp.arange(n // SWA_BLOCK))
    y = o.transpose(1, 0, 2, 3, 4, 5).reshape(b, n, dq) @ w_o
    yc = None
    if need_ctx:
        qc = (hc @ w_qkv[:, :dq]).reshape(b, L, SWA_KV_HEADS, grp, SWA_HEAD_DIM)
        s = jnp.einsum('bqhgd,bkhd->bhgqk', qc, kc, preferred_element_type=jnp.float32) * scale
        s_sink = jnp.broadcast_to(sink_g[None, :, :, None, None], s.shape[:-1] + (1,))
        p = jax.nn.softmax(jnp.concatenate([s, s_sink], axis=-1), axis=-1)[..., :L].astype(vc.dtype)
        yc = jnp.einsum('bhgqk,bkhd->bqhgd', p, vc).reshape(b, L, dq) @ w_o
    return y, yc


def mixer_na(h, hc, w_qkv, rel_bias, w_o, need_ctx):
    b, n, _ = h.shape
    L = hc.shape[1]
    rows = n // GRID_W
    kh = min(NA_ROWS, rows)
    dm = NA_HEADS * NA_HEAD_DIM
    scale = NA_HEAD_DIM ** -0.5
    q, k, v = jnp.split(h @ w_qkv, [dm, 2 * dm], axis=-1)
    grid = (b, rows, GRID_W, NA_HEADS, NA_HEAD_DIM)
    q, k, v = q.reshape(grid), k.reshape(grid), v.reshape(grid)
    kc, vc = jnp.split(hc @ w_qkv[:, dm:], [dm], axis=-1)
    kc = kc.reshape(b, L, NA_HEADS, NA_HEAD_DIM)
    vc = vc.reshape(b, L, NA_HEADS, NA_HEAD_DIM)
    col = jnp.arange(GRID_W)
    col_start = jnp.clip(col - NA_COLS // 2, 0, GRID_W - NA_COLS)
    col_valid = (col[None, :] >= col_start[:, None]) & (col[None, :] < col_start[:, None] + NA_COLS)
    dc_idx = jnp.clip(col[None, :] - col[:, None] + NA_COLS - 1, 0, 2 * NA_COLS - 2)
    bias_cols = rel_bias.astype(jnp.float32)[:, :, dc_idx]
    bias_cols = jnp.where(col_valid, bias_cols, NEG_INF)

    def row_block(r):
        rs = jnp.clip(r - kh // 2, 0, rows - kh)
        qi = lax.dynamic_index_in_dim(q, r, axis=1, keepdims=False)
        ki = lax.dynamic_slice_in_dim(k, rs, kh, axis=1)
        vi = lax.dynamic_slice_in_dim(v, rs, kh, axis=1)
        dr_idx = rs + jnp.arange(kh) - r + NA_ROWS - 1
        bias = jnp.take(bias_cols, dr_idx, axis=1).transpose(0, 2, 1, 3)
        s_loc = jnp.einsum('bqhd,bjkhd->bhqjk', qi, ki, preferred_element_type=jnp.float32) * scale + bias[None]
        s_loc = s_loc.reshape(b, NA_HEADS, GRID_W, kh * GRID_W)
        s_ctx = jnp.einsum('bqhd,blhd->bhql', qi, kc, preferred_element_type=jnp.float32) * scale
        p = jax.nn.softmax(jnp.concatenate([s_loc, s_ctx], axis=-1), axis=-1)
        p_loc = p[..., :kh * GRID_W].reshape(b, NA_HEADS, GRID_W, kh, GRID_W).astype(v.dtype)
        p_ctx = p[..., kh * GRID_W:].astype(v.dtype)
        return (jnp.einsum('bhqjk,bjkhd->bqhd', p_loc, vi)
                + jnp.einsum('bhql,blhd->bqhd', p_ctx, vc))

    o = lax.map(row_block, jnp.arange(rows))
    y = o.transpose(1, 0, 2, 3, 4).reshape(b, n, dm) @ w_o
    yc = None
    if need_ctx:
        qc = (hc @ w_qkv[:, :dm]).reshape(b, L, NA_HEADS, NA_HEAD_DIM)
        yc = ctx_attention(qc, kc, vc, scale).reshape(b, L, dm) @ w_o
    return y, yc


def sq_relu_mlp(h, w1, w2):
    return jnp.square(jax.nn.relu(h @ w1)) @ w2


def setup_inputs(seed: int = 0) -> dict:
    key = jax.random.key(seed)
    ks = jax.random.split(key, 24)

    def w(k, shape, fan_in, gain=1.0):
        return jax.random.normal(k, shape, jnp.float32) * (gain * fan_in ** -0.5)

    def gn(k, shape):
        return 1.0 + 0.05 * jax.random.normal(k, shape, jnp.float32)

    D = D_MODEL
    return {
        'x': jax.random.normal(ks[0], (BATCH, SEQ, D), jnp.float32),
        'c': jax.random.normal(ks[1], (BATCH, D), jnp.float32),
        'ctx': jax.random.normal(ks[2], (BATCH, CTX_LEN, D), jnp.float32),
        'c_ctx': jax.random.normal(ks[3], (D,), jnp.float32),
        'ada_w': w(ks[4], (DEPTH, D, N_MOD * D), D, 0.5),
        'ada_b': 0.02 * jax.random.normal(ks[5], (DEPTH, N_MOD * D), jnp.float32),
        'norm_mix_g': gn(ks[6], (DEPTH, D)),
        'norm_mlp_g': gn(ks[7], (DEPTH, D)),
        'norm_out_g': gn(ks[8], (D,)),
        'mlp_w1': w(ks[9], (DEPTH, D, D_FF), D),
        'mlp_w2': w(ks[10], (DEPTH, D_FF, D), D_FF),
        'mla_w_in': w(ks[11], (N_A_LAYERS, D, MLA_Q_RANK + MLA_KV_RANK + MLA_ROPE), D),
        'mla_qa_g': gn(ks[12], (N_A_LAYERS, MLA_Q_RANK)),
        'mla_w_qb': w(ks[13], (N_A_LAYERS, MLA_Q_RANK, MLA_HEADS * (MLA_NOPE + MLA_ROPE)), MLA_Q_RANK),
        'mla_kva_g': gn(ks[14], (N_A_LAYERS, MLA_KV_RANK)),
        'mla_w_kvb': w(ks[15], (N_A_LAYERS, MLA_KV_RANK, MLA_HEADS * (MLA_NOPE + MLA_V)), MLA_KV_RANK),
        'mla_w_o': w(ks[16], (N_A_LAYERS, MLA_HEADS * MLA_V, D), MLA_HEADS * MLA_V),
        'swa_w_qkv': w(ks[17], (N_B_LAYERS, D, (SWA_Q_HEADS + 2 * SWA_KV_HEADS) * SWA_HEAD_DIM), D),
        'swa_sink': 0.5 * jax.random.normal(ks[18], (N_B_LAYERS, SWA_Q_HEADS), jnp.float32),
        'swa_w_o': w(ks[19], (N_B_LAYERS, SWA_Q_HEADS * SWA_HEAD_DIM, D), SWA_Q_HEADS * SWA_HEAD_DIM),
        'na_w_qkv': w(ks[20], (N_C_LAYERS, D, 3 * NA_HEADS * NA_HEAD_DIM), D),
        'na_rel_bias': 0.5 * jax.random.normal(ks[21], (N_C_LAYERS, NA_HEADS, 2 * NA_ROWS - 1, 2 * NA_COLS - 1), jnp.float32),
        'na_w_o': w(ks[22], (N_C_LAYERS, NA_HEADS * NA_HEAD_DIM, D), NA_HEADS * NA_HEAD_DIM),
    }


def reference(x, c, ctx, c_ctx, ada_w, ada_b, norm_mix_g, norm_mlp_g, norm_out_g, mlp_w1, mlp_w2,
              mla_w_in, mla_qa_g, mla_w_qb, mla_kva_g, mla_w_kvb, mla_w_o,
              swa_w_qkv, swa_sink, swa_w_o, na_w_qkv, na_rel_bias, na_w_o):
    D = D_MODEL
    c_act = jax.nn.silu(c)
    cc_act = jax.nn.silu(c_ctx)
    for i in range(DEPTH):
        need_ctx = i < DEPTH - 1
        mod = c_act @ ada_w[i] + ada_b[i]
        sh1, sc1, g1, sh2, sc2, g2 = jnp.split(mod[:, None, :], N_MOD, axis=-1)
        n_mod_c = N_MOD if need_ctx else 3
        mod_c = cc_act @ ada_w[i][:, :n_mod_c * D] + ada_b[i][:n_mod_c * D]
        mods_c = jnp.split(mod_c, n_mod_c, axis=-1)
        h = modulate(rmsnorm(x, norm_mix_g[i]), sh1, sc1)
        hc = modulate(rmsnorm(ctx, norm_mix_g[i]), mods_c[0], mods_c[1])
        kind, j = i % N_MIXERS, i // N_MIXERS
        if kind == 0:
            y, yc = mixer_mla(h, hc, mla_w_in[j], mla_qa_g[j], mla_w_qb[j], mla_kva_g[j], mla_w_kvb[j], mla_w_o[j], need_ctx)
        elif kind == 1:
            y, yc = mixer_swa(h, hc, swa_w_qkv[j], swa_sink[j], swa_w_o[j], need_ctx)
        else:
            y, yc = mixer_na(h, hc, na_w_qkv[j], na_rel_bias[j], na_w_o[j], need_ctx)
        x = x + g1 * y
        x = x + g2 * sq_relu_mlp(modulate(rmsnorm(x, norm_mlp_g[i]), sh2, sc2), mlp_w1[i], mlp_w2[i])
        if need_ctx:
            ctx = ctx + mods_c[2] * yc
            hc2 = modulate(rmsnorm(ctx, norm_mlp_g[i]), mods_c[3], mods_c[4])
            ctx = ctx + mods_c[5] * sq_relu_mlp(hc2, mlp_w1[i], mlp_w2[i])
    return rmsnorm(x, norm_out_g)
```

```python
import functools

import numpy as np
import jax
import jax.numpy as jnp
from jax import lax
from jax.experimental import pallas as pl
from jax.experimental.pallas import tpu as pltpu

F32 = jnp.float32
BF16 = jnp.bfloat16

D_MODEL = 1024
D_FF = 4 * D_MODEL
DEPTH = 4
N_MOD = 6
GRID_W = 64
EPS = 1e-6
ROPE_BASE = 10000.0
NEG_INF = -1e30

MLA_HEADS = 16
MLA_Q_RANK = 256
MLA_KV_RANK = 128
MLA_NOPE = 64
MLA_ROPE = 32
MLA_V = 64

SWA_Q_HEADS = 16
SWA_KV_HEADS = 4
SWA_HEAD_DIM = 64
SWA_WINDOW = 128
SWA_BLOCK = 128

NA_HEADS = 16
NA_HEAD_DIM = 64
NA_ROWS = 8
NA_COLS = 16

LANES = 128
HEAD_V = 64
VMEM_LIMIT = 56 * 1024 * 1024


def _cparams(n_axes):
    return pltpu.CompilerParams(dimension_semantics=("arbitrary",) * n_axes,
                                vmem_limit_bytes=VMEM_LIMIT)


def _const_spec(shape):
    nd = len(shape)
    return pl.BlockSpec(shape, lambda *_: (0,) * nd, pipeline_mode=pl.Buffered(1))


def _rms(x):
    return x * lax.rsqrt(jnp.mean(x * x, axis=-1, keepdims=True) + EPS)


def _norm_mod(x, g, shift, scale):
    return (_rms(x) * g) * (1.0 + scale) + shift


def _pair_heads(o_even, o_odd):
    lane = lax.broadcasted_iota(jnp.int32, o_even.shape, 1)
    return jnp.where(lane < HEAD_V, o_even, pltpu.roll(o_odd, HEAD_V, 1))


def _ada_kernel(c_ref, w_ref, b_ref, o_ref):
    c = c_ref[...]
    act = c / (1.0 + jnp.exp(-c))
    o_ref[0] = jnp.dot(act.astype(BF16), w_ref[0], preferred_element_type=F32) + b_ref[0]


def _ada_mods(c_rows, ada_w, ada_b):
    depth, d, n = ada_w.shape
    tn = 1536
    return pl.pallas_call(
        _ada_kernel,
        grid=(depth, n // tn),
        in_specs=[
            pl.BlockSpec((c_rows.shape[0], d), lambda i, j: (0, 0)),
            pl.BlockSpec((1, d, tn), lambda i, j: (i, 0, j)),
            pl.BlockSpec((1, 1, tn), lambda i, j: (i, 0, j)),
        ],
        out_specs=pl.BlockSpec((1, c_rows.shape[0], tn), lambda i, j: (i, 0, j)),
        out_shape=jax.ShapeDtypeStruct((depth, c_rows.shape[0], n), F32),
        compiler_params=_cparams(2),
        name="ada_mods",
    )(c_rows, ada_w.astype(BF16), ada_b.reshape(depth, 1, n))


class _Geom:
    def __init__(self, b, n, l):
        self.b, self.n, self.l = b, n, l
        self.rows_x = b * n
        self.rows_c = b * l
        self.rows = self.rows_x + self.rows_c
        tm = 512
        while self.rows_c % tm or n % tm:
            tm //= 2
        self.tm = tm
        self.tiles_x = self.rows_x // tm
        self.tiles = self.rows // tm
        self.tiles_per_batch = n // tm

    def mod_row(self, t):
        return jnp.where(t < self.tiles_x, t // self.tiles_per_batch, self.b)

    def pos_block(self, t):
        return jnp.where(t < self.tiles_x, t % self.tiles_per_batch, self.tiles_per_batch)


def _mod_spec(geom, layer):
    return pl.BlockSpec((None, None, 1, N_MOD * D_MODEL), lambda t: (layer, geom.mod_row(t), 0, 0))


def _row_spec(geom, width):
    return pl.BlockSpec((geom.tm, width), lambda t: (t, 0))


def _pos_spec(geom):
    return pl.BlockSpec((geom.tm, LANES), lambda t: (geom.pos_block(t), 0))


def _rope_cos_sin(n, rot_dim):
    t = jnp.arange(n, dtype=jnp.int32)
    row = (t // GRID_W).astype(F32)
    col = (t % GRID_W).astype(F32)
    n_freq = rot_dim // 4
    inv_freq = ROPE_BASE ** (-jnp.arange(n_freq, dtype=F32) / n_freq)
    ang = jnp.concatenate([row[:, None] * inv_freq, col[:, None] * inv_freq], axis=-1)
    return jnp.cos(ang), jnp.sin(ang)


def _rope_tables(geom, rot_dim, lead, scale):
    c, s = _rope_cos_sin(geom.n, rot_dim)
    half = rot_dim // 2
    pad = LANES - lead - rot_dim
    n, tm = geom.n, geom.tm
    cos_t = jnp.concatenate([jnp.ones((n, lead), F32), c, c, jnp.zeros((n, pad), F32)], axis=1)
    sin_t = jnp.concatenate([jnp.zeros((n, lead), F32), -s, s, jnp.zeros((n, pad), F32)], axis=1)
    cos_c = jnp.concatenate([jnp.ones((tm, lead + 2 * half), F32), jnp.zeros((tm, pad), F32)], axis=1)
    sin_c = jnp.zeros((tm, LANES), F32)
    return (jnp.concatenate([cos_t, cos_c], axis=0) * scale,
            jnp.concatenate([sin_t, sin_c], axis=0) * scale)


def _ones_lane_row(n_slots):
    row = np.zeros((1, n_slots, LANES), np.float32)
    row[:, :, HEAD_V] = 1.0
    return jnp.asarray(row.reshape(1, n_slots * LANES))


def _prep_mla_weights(w_in, w_qb, w_kvb):
    d = w_in.shape[0]
    pe0 = MLA_Q_RANK + MLA_KV_RANK
    ev = pe0 + 2 * np.arange(MLA_ROPE // 2)
    od = ev + 1
    z = jnp.zeros((d, LANES - MLA_ROPE), F32)
    win = jnp.concatenate([w_in[:, :pe0], w_in[:, ev], w_in[:, od], z, w_in[:, od], w_in[:, ev], z], axis=1)

    hd = MLA_NOPE + MLA_ROPE
    wq3 = w_qb.reshape(MLA_Q_RANK, MLA_HEADS, hd)
    nope, pe_e, pe_o = wq3[:, :, :MLA_NOPE], wq3[:, :, MLA_NOPE::2], wq3[:, :, MLA_NOPE + 1::2]
    zpad = jnp.zeros((MLA_Q_RANK, MLA_HEADS, LANES - hd), F32)
    zlead = jnp.zeros((MLA_Q_RANK, MLA_HEADS, MLA_NOPE), F32)
    wqa = jnp.concatenate([nope, pe_e, pe_o, zpad], axis=-1).reshape(MLA_Q_RANK, MLA_HEADS * LANES)
    wqb = jnp.concatenate([zlead, pe_o, pe_e, zpad], axis=-1).reshape(MLA_Q_RANK, MLA_HEADS * LANES)
    wq = jnp.concatenate([wqa, wqb], axis=1)

    wkv3 = w_kvb.reshape(MLA_KV_RANK, MLA_HEADS, MLA_NOPE + MLA_V)
    zk = jnp.zeros((MLA_KV_RANK, MLA_HEADS, LANES - MLA_NOPE), F32)
    wk_top = jnp.concatenate([wkv3[:, :, :MLA_NOPE], zk], axis=-1).reshape(MLA_KV_RANK, MLA_HEADS * LANES)
    eye = np.zeros((LANES, MLA_HEADS, LANES), np.float32)
    for j in range(MLA_ROPE):
        eye[j, :, MLA_NOPE + j] = 1.0
    wk = jnp.concatenate([wk_top, jnp.asarray(eye.reshape(LANES, MLA_HEADS * LANES))], axis=0)
    zv = jnp.zeros((MLA_KV_RANK, MLA_HEADS, LANES - MLA_V), F32)
    wv_top = jnp.concatenate([wkv3[:, :, MLA_NOPE:], zv], axis=-1).reshape(MLA_KV_RANK, MLA_HEADS * LANES)
    wv = jnp.concatenate([wv_top, jnp.zeros((LANES, MLA_HEADS * LANES), F32)], axis=0)
    wkv = jnp.concatenate([wk, wv], axis=1)
    return win.astype(BF16), wq.astype(BF16), wkv.astype(BF16)


def _mla_pre_kernel(x_ref, mod_ref, g_ref, win_ref, qag_ref, kvag_ref, wq_ref, wkv_ref,
                    cq_ref, sq_ref, ck_ref, sk_ref, ones_ref, q_out, k_out, v_out):
    mod = mod_ref[...]
    h = _norm_mod(x_ref[...], g_ref[...], mod[:, :D_MODEL], mod[:, D_MODEL:2 * D_MODEL]).astype(BF16)
    proj = jnp.dot(h, win_ref[...], preferred_element_type=F32)
    kv0 = MLA_Q_RANK
    pe0 = kv0 + MLA_KV_RANK
    qn = (_rms(proj[:, :kv0]) * qag_ref[...]).astype(BF16)
    kvn = _rms(proj[:, kv0:pe0]) * kvag_ref[...]
    qab = jnp.dot(qn, wq_ref[...], preferred_element_type=F32)
    cq, sq = cq_ref[...], sq_ref[...]
    half = MLA_HEADS * LANES
    for hh in range(MLA_HEADS):
        lo = hh * LANES
        q_out[:, lo:lo + LANES] = (qab[:, lo:lo + LANES] * cq
                                   + qab[:, half + lo:half + lo + LANES] * sq).astype(BF16)
    kpe = proj[:, pe0:pe0 + LANES] * ck_ref[...] + proj[:, pe0 + LANES:pe0 + 2 * LANES] * sk_ref[...]
    lat = jnp.concatenate([kvn, kpe], axis=1).astype(BF16)
    kv = jnp.dot(lat, wkv_ref[...], preferred_element_type=F32)
    k_out[...] = kv[:, :half].astype(BF16)
    v_out[...] = (kv[:, half:] + ones_ref[...]).astype(BF16)


def _mla_pre(geom, layer, xall, mods, g, w_in, qa_g, w_qb, kva_g, w_kvb):
    win, wq, wkv = _prep_mla_weights(w_in, w_qb, w_kvb)
    scale = (MLA_NOPE + MLA_ROPE) ** -0.5
    cq, sq = _rope_tables(geom, MLA_ROPE, MLA_NOPE, scale)
    ck, sk = _rope_tables(geom, MLA_ROPE, 0, 1.0)
    width = MLA_HEADS * LANES
    out = jax.ShapeDtypeStruct((geom.rows, width), BF16)
    return pl.pallas_call(
        _mla_pre_kernel,
        grid=(geom.tiles,),
        in_specs=[
            _row_spec(geom, D_MODEL), _mod_spec(geom, layer), _const_spec((1, D_MODEL)),
            _const_spec(win.shape), _const_spec((1, MLA_Q_RANK)), _const_spec((1, MLA_KV_RANK)),
            _const_spec(wq.shape), _const_spec(wkv.shape),
            _pos_spec(geom), _pos_spec(geom), _pos_spec(geom), _pos_spec(geom),
            _const_spec((1, width)),
        ],
        out_specs=[_row_spec(geom, width)] * 3,
        out_shape=[out, out, out],
        compiler_params=_cparams(1),
        name="mla_pre",
    )(xall, mods, g.reshape(1, D_MODEL), win, qa_g.reshape(1, -1), kva_g.reshape(1, -1), wq, wkv,
      cq, sq, ck, sk, _ones_lane_row(MLA_HEADS))


def _mla_attn_kernel(q_ref, kx_ref, kc_ref, vx_ref, vc_ref, o_ref, *, tk, n_chunks, tiles_x):
    t = pl.program_id(2)
    trips = jnp.where(t < tiles_x, n_chunks, 0)
    dn = (((1,), (1,)), ((), ()))
    outs = []
    for j in range(2):
        ls = slice(j * LANES, (j + 1) * LANES)
        q = q_ref[:, ls]
        s = lax.dot_general(q, kc_ref[:, ls], dn, preferred_element_type=F32)
        m = jnp.max(s, axis=-1, keepdims=True)
        p = jnp.exp(s - m)
        acc = jnp.dot(p.astype(BF16), vc_ref[:, ls], preferred_element_type=F32)

        def body(c, carry, q=q, ls=ls):
            m, acc = carry
            off = pl.multiple_of(c * tk, tk)
            k = kx_ref[pl.ds(off, tk), ls]
            v = vx_ref[pl.ds(off, tk), ls]
            s = lax.dot_general(q, k, dn, preferred_element_type=F32)
            m_new = jnp.maximum(m, jnp.max(s, axis=-1, keepdims=True))
            alpha = jnp.exp(m - m_new)
            p = jnp.exp(s - m_new)
            acc = alpha * acc + jnp.dot(p.astype(BF16), v, preferred_element_type=F32)
            return m_new, acc

        m, acc = lax.fori_loop(0, trips, body, (m, acc))
        outs.append(acc * (1.0 / acc[:, HEAD_V:HEAD_V + 1]))
    o_ref[...] = _pair_heads(outs[0], outs[1]).astype(BF16)


def _mla_attn(geom, q, k, v, need_ctx):
    b, n, l = geom.b, geom.n, geom.l
    tq = l
    tk = min(512, n)
    tiles_x = n // tq
    nt = tiles_x + (1 if need_ctx else 0)
    ctx_q0 = geom.rows_x // tq
    ctx_k0 = geom.rows_x // l

    def q_map(bi, hp, t):
        return (jnp.where(t < tiles_x, bi * tiles_x + t, ctx_q0 + bi), hp)

    kx_spec = pl.BlockSpec((n, 2 * LANES), lambda bi, hp, t: (bi, hp))
    kc_spec = pl.BlockSpec((l, 2 * LANES), lambda bi, hp, t: (ctx_k0 + bi, hp))
    return pl.pallas_call(
        functools.partial(_mla_attn_kernel, tk=tk, n_chunks=n // tk, tiles_x=tiles_x),
        grid=(b, MLA_HEADS // 2, nt),
        in_specs=[pl.BlockSpec((tq, 2 * LANES), q_map), kx_spec, kc_spec, kx_spec, kc_spec],
        out_specs=pl.BlockSpec((tq, LANES), q_map),
        out_shape=jax.ShapeDtypeStruct((geom.rows, MLA_HEADS * MLA_V), BF16),
        compiler_params=_cparams(3),
        name="mla_attn",
    )(q, k, k, v, v)


def _prep_swa_weights(w_qkv):
    d = w_qkv.shape[0]
    dq = SWA_Q_HEADS * SWA_HEAD_DIM
    dkv = SWA_KV_HEADS * SWA_HEAD_DIM

    def rope_pair(w, heads):
        w3 = w.reshape(d, heads, SWA_HEAD_DIM)
        e, o = w3[:, :, 0::2], w3[:, :, 1::2]
        z = jnp.zeros((d, heads, LANES - SWA_HEAD_DIM), F32)
        return (jnp.concatenate([e, o, z], axis=-1).reshape(d, heads * LANES),
                jnp.concatenate([o, e, z], axis=-1).reshape(d, heads * LANES))

    wqa, wqb = rope_pair(w_qkv[:, :dq], SWA_Q_HEADS)
    wka, wkb = rope_pair(w_qkv[:, dq:dq + dkv], SWA_KV_HEADS)
    v3 = w_qkv[:, dq + dkv:].reshape(d, SWA_KV_HEADS, SWA_HEAD_DIM)
    wv = jnp.concatenate([v3, jnp.zeros((d, SWA_KV_HEADS, LANES - SWA_HEAD_DIM), F32)], axis=-1)
    wv = wv.reshape(d, SWA_KV_HEADS * LANES)
    return jnp.concatenate([wqa, wqb, wka, wkb, wv], axis=1).astype(BF16)


def _swa_pre_kernel(x_ref, mod_ref, g_ref, w_ref, cq_ref, sq_ref, ck_ref, sk_ref, ones_ref,
                    q_out, k_out, v_out):
    mod = mod_ref[...]
    h = _norm_mod(x_ref[...], g_ref[...], mod[:, :D_MODEL], mod[:, D_MODEL:2 * D_MODEL]).astype(BF16)
    qw = SWA_Q_HEADS * LANES
    kw = SWA_KV_HEADS * LANES
    qab = jnp.dot(h, w_ref[:, :2 * qw], preferred_element_type=F32)
    cq, sq = cq_ref[...], sq_ref[...]
    for hh in range(SWA_Q_HEADS):
        lo = hh * LANES
        q_out[:, lo:lo + LANES] = (qab[:, lo:lo + LANES] * cq
                                   + qab[:, qw + lo:qw + lo + LANES] * sq).astype(BF16)
    kv = jnp.dot(h, w_ref[:, 2 * qw:], preferred_element_type=F32)
    ck, sk = ck_ref[...], sk_ref[...]
    for hh in range(SWA_KV_HEADS):
        lo = hh * LANES
        k_out[:, lo:lo + LANES] = (kv[:, lo:lo + LANES] * ck
                                   + kv[:, kw + lo:kw + lo + LANES] * sk).astype(BF16)
    v_out[...] = (kv[:, 2 * kw:] + ones_ref[...]).astype(BF16)


def _swa_pre(geom, layer, xall, mods, g, w_qkv):
    w = _prep_swa_weights(w_qkv)
    scale = SWA_HEAD_DIM ** -0.5
    cq, sq = _rope_tables(geom, SWA_HEAD_DIM, 0, scale)
    ck, sk = _rope_tables(geom, SWA_HEAD_DIM, 0, 1.0)
    qw, kw = SWA_Q_HEADS * LANES, SWA_KV_HEADS * LANES
    return pl.pallas_call(
        _swa_pre_kernel,
        grid=(geom.tiles,),
        in_specs=[
            _row_spec(geom, D_MODEL), _mod_spec(geom, layer), _const_spec((1, D_MODEL)),
            _const_spec(w.shape),
            _pos_spec(geom), _pos_spec(geom), _pos_spec(geom), _pos_spec(geom),
            _const_spec((1, kw)),
        ],
        out_specs=[_row_spec(geom, qw), _row_spec(geom, kw), _row_spec(geom, kw)],
        out_shape=[jax.ShapeDtypeStruct((geom.rows, qw), BF16),
                   jax.ShapeDtypeStruct((geom.rows, kw), BF16),
                   jax.ShapeDtypeStruct((geom.rows, kw), BF16)],
        compiler_params=_cparams(1),
        name="swa_pre",
    )(xall, mods, g.reshape(1, D_MODEL), w, cq, sq, ck, sk, _ones_lane_row(SWA_KV_HEADS))


def _prep_na_weights(w_qkv):
    d = w_qkv.shape[0]
    dm = NA_HEADS * NA_HEAD_DIM
    z = jnp.zeros((d, NA_HEADS, LANES - NA_HEAD_DIM), F32)

    def slots(w):
        return jnp.concatenate([w.reshape(d, NA_HEADS, NA_HEAD_DIM), z], axis=-1).reshape(d, NA_HEADS * LANES)

    scale = NA_HEAD_DIM ** -0.5
    return jnp.concatenate([slots(w_qkv[:, :dm]) * scale, slots(w_qkv[:, dm:2 * dm]),
                            slots(w_qkv[:, 2 * dm:])], axis=1).astype(BF16)


def _na_pre_kernel(x_ref, mod_ref, g_ref, w_ref, ones_ref, q_out, k_out, v_out):
    mod = mod_ref[...]
    h = _norm_mod(x_ref[...], g_ref[...], mod[:, :D_MODEL], mod[:, D_MODEL:2 * D_MODEL]).astype(BF16)
    hw = NA_HEADS * LANES
    q_out[...] = jnp.dot(h, w_ref[:, :hw], preferred_element_type=F32).astype(BF16)
    k_out[...] = jnp.dot(h, w_ref[:, hw:2 * hw], preferred_element_type=F32).astype(BF16)
    v_out[...] = (jnp.dot(h, w_ref[:, 2 * hw:], preferred_element_type=F32) + ones_ref[...]).astype(BF16)


def _na_pre(geom, layer, xall, mods, g, w_qkv):
    w = _prep_na_weights(w_qkv)
    hw = NA_HEADS * LANES
    out = jax.ShapeDtypeStruct((geom.rows, hw), BF16)
    return pl.pallas_call(
        _na_pre_kernel,
        grid=(geom.tiles,),
        in_specs=[_row_spec(geom, D_MODEL), _mod_spec(geom, layer), _const_spec((1, D_MODEL)),
                  _const_spec(w.shape), _const_spec((1, hw))],
        out_specs=[_row_spec(geom, hw)] * 3,
        out_shape=[out, out, out],
        compiler_params=_cparams(1),
        name="na_pre",
    )(xall, mods, g.reshape(1, D_MODEL), w, _ones_lane_row(NA_HEADS))


def _band_kernel(*refs, tq, hbq, grp, has_sink, head0_stride):
    if has_sink:
        q_ref, kp, kc, kn, kx, vp, vc, vn, vx, bias_ref, sink_ref, o_ref = refs
    else:
        q_ref, kp, kc, kn, kx, vp, vc, vn, vx, bias_ref, o_ref = refs
    hbk = hbq // grp
    shared_bias = bias_ref.shape[1] == 1
    dn = (((1,), (1,)), ((), ()))
    head0 = pl.program_id(1) * head0_stride
    outs = []
    for j in range(hbk):
        ls = slice(j * LANES, (j + 1) * LANES)
        kcat = jnp.concatenate([kp[:, ls], kc[:, ls], kn[:, ls], kx[:, ls]], axis=0)
        vcat = jnp.concatenate([vp[:, ls], vc[:, ls], vn[:, ls], vx[:, ls]], axis=0)
        width = kcat.shape[0]
        if grp > 1:
            qs = jnp.concatenate([q_ref[:, (j * grp + i) * LANES:(j * grp + i + 1) * LANES]
                                  for i in range(grp)], axis=0)
        else:
            qs = q_ref[:, ls]
        s = lax.dot_general(qs, kcat, dn, preferred_element_type=F32)
        if shared_bias:
            s = (s.reshape(grp, tq, width) + bias_ref[0, 0][None]).reshape(grp * tq, width)
        else:
            s = s + bias_ref[0, j]
        m = jnp.max(s, axis=-1, keepdims=True)
        if has_sink:
            row_head = lax.broadcasted_iota(jnp.int32, (grp * tq, 1), 0) // tq
            sink_col = jnp.zeros((grp * tq, 1), F32)
            for i in range(grp):
                sink_col = jnp.where(row_head == i, sink_ref[head0 + j * grp + i], sink_col)
            m = jnp.maximum(m, sink_col)
        p = jnp.exp(s - m)
        acc = jnp.dot(p.astype(BF16), vcat, preferred_element_type=F32)
        denom = acc[:, HEAD_V:HEAD_V + 1]
        if has_sink:
            denom = denom + jnp.exp(sink_col - m)
        o = acc * (1.0 / denom)
        for i in range(grp):
            outs.append(o[i * tq:(i + 1) * tq])
    for pi in range(hbq // 2):
        o_ref[:, pi * LANES:(pi + 1) * LANES] = _pair_heads(outs[2 * pi], outs[2 * pi + 1]).astype(BF16)


def _band_attn(geom, q, k, v, bias, sink, *, tq, hbq, grp, need_ctx, name):
    b, n, l = geom.b, geom.n, geom.l
    n_q_heads = q.shape[1] // LANES
    hbk = hbq // grp
    n_groups = n_q_heads // hbq
    tiles_x = n // tq
    ctx_tiles = l // tq
    nt = tiles_x + (ctx_tiles if need_ctx else 0)
    ctx_q0 = geom.rows_x // tq
    ctx_k0 = geom.rows_x // l

    def q_map(bi, g, t):
        return (jnp.where(t < tiles_x, bi * tiles_x + t, ctx_q0 + bi * ctx_tiles + (t - tiles_x)), g)

    def band_map(delta):
        def index(bi, g, t):
            tt = jnp.clip(jnp.minimum(t, tiles_x - 1) + delta, 0, tiles_x - 1)
            return (bi * tiles_x + tt, g)
        return index

    def bias_map(bi, g, t):
        variant = jnp.where(t >= tiles_x, 3, jnp.where(t == 0, 0, jnp.where(t == tiles_x - 1, 2, 1)))
        return (variant, g if bias.shape[1] > 1 else 0, 0, 0)

    band_specs = [pl.BlockSpec((tq, hbk * LANES), band_map(dl)) for dl in (-1, 0, 1)]
    ctx_spec = pl.BlockSpec((l, hbk * LANES), lambda bi, g, t: (ctx_k0 + bi, g))
    hb_bias = hbq if bias.shape[1] > 1 else 1
    in_specs = ([pl.BlockSpec((tq, hbq * LANES), q_map)] + band_specs + [ctx_spec] + band_specs + [ctx_spec]
                + [pl.BlockSpec((1, hb_bias, tq, 3 * tq + l), bias_map)])
    args = [q, k, k, k, k, v, v, v, v, bias]
    if sink is not None:
        in_specs.append(pl.BlockSpec(memory_space=pltpu.SMEM))
        args.append(sink)
    return pl.pallas_call(
        functools.partial(_band_kernel, tq=tq, hbq=hbq, grp=grp, has_sink=sink is not None,
                          head0_stride=hbq),
        grid=(b, n_groups, nt),
        in_specs=in_specs,
        out_specs=pl.BlockSpec((tq, hbq * HEAD_V), q_map),
        out_shape=jax.ShapeDtypeStruct((geom.rows, n_q_heads * HEAD_V), BF16),
        compiler_params=_cparams(3),
        name=name,
    )(*args)


def _swa_bias(l):
    tq = SWA_BLOCK
    i = np.arange(tq)[:, None]
    j = np.arange(tq)[None, :]
    prev_ok = (SWA_BLOCK + i - j) <= SWA_WINDOW
    next_ok = (SWA_BLOCK + j - i) <= SWA_WINDOW
    cur_ok = np.abs(i - j) <= SWA_WINDOW
    none = np.zeros((tq, tq), bool)
    ctx_ok = np.ones((tq, l), bool)
    variants = [
        np.concatenate([none, cur_ok, next_ok, ctx_ok], axis=1),
        np.concatenate([prev_ok, cur_ok, next_ok, ctx_ok], axis=1),
        np.concatenate([prev_ok, cur_ok, none, ctx_ok], axis=1),
        np.concatenate([none, none, none, ctx_ok], axis=1),
    ]
    ok = np.stack(variants)[:, None]
    return jnp.asarray(np.where(ok, 0.0, NEG_INF).astype(np.float32))


NA_TILE_ROWS = 4


def _na_bias(rel_bias, l):
    col = np.arange(GRID_W)
    col_start = np.clip(col - NA_COLS // 2, 0, GRID_W - NA_COLS)
    col_valid = (col[None, :] >= col_start[:, None]) & (col[None, :] < col_start[:, None] + NA_COLS)
    dc_idx = np.clip(col[None, :] - col[:, None] + NA_COLS - 1, 0, 2 * NA_COLS - 2)
    bias_cols = rel_bias.astype(F32)[:, :, dc_idx]
    bias_cols = jnp.where(col_valid, bias_cols, NEG_INF)
    r = NA_TILE_ROWS
    a = np.arange(r)[:, None]
    c = np.arange(3 * r)[None, :]
    dr_idx = np.clip(c - r - a + NA_ROWS - 1, 0, 2 * NA_ROWS - 2)
    tiles = bias_cols[:, dr_idx]
    tiles = tiles.transpose(0, 1, 3, 2, 4)
    half = NA_ROWS // 2
    interior = (c - r >= a - half) & (c - r < a - half + NA_ROWS)
    first = (c >= r) & (c < r + NA_ROWS) & (a >= 0)
    last = (c >= 2 * r - NA_ROWS) & (c < 2 * r) & (a >= 0)
    none = np.zeros_like(interior)
    h = rel_bias.shape[0]
    out = []
    for ok in (first, interior, last, none):
        ok5 = ok[None, :, None, :, None]
        t = jnp.where(ok5, tiles, NEG_INF).reshape(h, r * GRID_W, 3 * r * GRID_W)
        out.append(jnp.concatenate([t, jnp.zeros((h, r * GRID_W, l), F32)], axis=-1))
    return jnp.stack(out)


def _post_kernel(*refs, final):
    if final:
        x_ref, o_ref, mod_ref, wo_ref, g_ref, w1_ref, w2_ref, gout_ref, out_ref = refs
    else:
        x_ref, o_ref, mod_ref, wo_ref, g_ref, w1_ref, w2_ref, out_ref = refs
    d = D_MODEL
    mod = mod_ref[...]
    g1, sh2, sc2, g2 = mod[:, 2 * d:3 * d], mod[:, 3 * d:4 * d], mod[:, 4 * d:5 * d], mod[:, 5 * d:6 * d]
    x1 = x_ref[...] + g1 * jnp.dot(o_ref[...], wo_ref[...], preferred_element_type=F32)
    h2 = _norm_mod(x1, g_ref[...], sh2, sc2).astype(BF16)
    acc = jnp.zeros(x1.shape, F32)
    for c in range(D_FF // d):
        u = jnp.maximum(jnp.dot(h2, w1_ref[:, c * d:(c + 1) * d], preferred_element_type=F32), 0.0)
        acc = acc + jnp.dot((u * u).astype(BF16), w2_ref[c * d:(c + 1) * d, :], preferred_element_type=F32)
    x2 = x1 + g2 * acc
    if final:
        x2 = _rms(x2) * gout_ref[...]
    out_ref[...] = x2


def _post(geom, layer, xall, o, mods, w_o, g, w1, w2, g_out):
    final = g_out is not None
    tiles = geom.tiles_x if final else geom.tiles
    rows = geom.rows_x if final else geom.rows
    in_specs = [_row_spec(geom, D_MODEL), _row_spec(geom, D_MODEL), _mod_spec(geom, layer),
                _const_spec(w_o.shape), _const_spec((1, D_MODEL)), _const_spec(w1.shape), _const_spec(w2.shape)]
    args = [xall, o, mods, w_o.astype(BF16), g.reshape(1, D_MODEL), w1.astype(BF16), w2.astype(BF16)]
    if final:
        in_specs.append(_const_spec((1, D_MODEL)))
        args.append(g_out.reshape(1, D_MODEL))
    return pl.pallas_call(
        functools.partial(_post_kernel, final=final),
        grid=(tiles,),
        in_specs=in_specs,
        out_specs=_row_spec(geom, D_MODEL),
        out_shape=jax.ShapeDtypeStruct((rows, D_MODEL), F32),
        compiler_params=_cparams(1),
        name="post_mlp",
    )(*args)


def kernel(x, c, ctx, c_ctx, ada_w, ada_b, norm_mix_g, norm_mlp_g, norm_out_g, mlp_w1, mlp_w2,
           mla_w_in, mla_qa_g, mla_w_qb, mla_kva_g, mla_w_kvb, mla_w_o,
           swa_w_qkv, swa_sink, swa_w_o, na_w_qkv, na_rel_bias, na_w_o):
    b, n, d = x.shape
    l = ctx.shape[1]
    assert d == D_MODEL and n % GRID_W == 0 and l % SWA_BLOCK == 0 and l == NA_TILE_ROWS * GRID_W
    geom = _Geom(b, n, l)

    mod_rows = -(-(b + 1) // 8) * 8
    c_rows = jnp.concatenate([c, c_ctx[None, :], jnp.zeros((mod_rows - b - 1, d), F32)], axis=0)
    mods = _ada_mods(c_rows, ada_w, ada_b).reshape(DEPTH, mod_rows, 1, N_MOD * d)

    xall = jnp.concatenate([x.reshape(b * n, d), ctx.reshape(b * l, d)], axis=0)
    for i in range(DEPTH):
        need_ctx = i < DEPTH - 1
        kind, j = i % 3, i // 3
        if kind == 0:
            q, k, v = _mla_pre(geom, i, xall, mods, norm_mix_g[i], mla_w_in[j], mla_qa_g[j], mla_w_qb[j],
                               mla_kva_g[j], mla_w_kvb[j])
            o = _mla_attn(geom, q, k, v, need_ctx)
            w_o = mla_w_o[j]
        elif kind == 1:
            q, k, v = _swa_pre(geom, i, xall, mods, norm_mix_g[i], swa_w_qkv[j])
            o = _band_attn(geom, q, k, v, _swa_bias(l), swa_sink[j], tq=SWA_BLOCK, hbq=SWA_Q_HEADS,
                           grp=SWA_Q_HEADS // SWA_KV_HEADS, need_ctx=need_ctx, name="swa_attn")
            w_o = swa_w_o[j]
        else:
            q, k, v = _na_pre(geom, i, xall, mods, norm_mix_g[i], na_w_qkv[j])
            o = _band_attn(geom, q, k, v, _na_bias(na_rel_bias[j], l), None, tq=NA_TILE_ROWS * GRID_W,
                           hbq=4, grp=1, need_ctx=need_ctx, name="na_attn")
            w_o = na_w_o[j]
        g_out = norm_out_g if i == DEPTH - 1 else None
        xall = _post(geom, i, xall, o, mods, w_o, norm_mlp_g[i], mlp_w1[i], mlp_w2[i], g_out)
    return xall.reshape(b, n, d)
```

```python
import functools
import math

import numpy as np
import jax
import jax.numpy as jnp
from jax import lax
from jax.experimental import pallas as pl
from jax.experimental.pallas import tpu as pltpu

F32 = jnp.float32
BF16 = jnp.bfloat16

D_MODEL = 1024
D_FF = 4 * D_MODEL
DEPTH = 4
N_MOD = 6
GRID_W = 64
EPS = 1e-6
ROPE_BASE = 10000.0
NEG_INF = -1e30

MLA_HEADS = 16
MLA_Q_RANK = 256
MLA_KV_RANK = 128
MLA_NOPE = 64
MLA_ROPE = 32
MLA_V = 64

SWA_Q_HEADS = 16
SWA_KV_HEADS = 4
SWA_HEAD_DIM = 64
SWA_WINDOW = 128
SWA_BLOCK = 128

NA_HEADS = 16
NA_HEAD_DIM = 64
NA_ROWS = 8
NA_COLS = 16

LANES = 128
HEAD_V = 64
V_ROWS = 80
VMEM_LIMIT = 56 * 1024 * 1024
LOG2E = math.log2(math.e)


def _cparams(n_axes):
    return pltpu.CompilerParams(dimension_semantics=("arbitrary",) * n_axes,
                                vmem_limit_bytes=VMEM_LIMIT)


def _const_spec(shape):
    nd = len(shape)
    return pl.BlockSpec(shape, lambda *_: (0,) * nd, pipeline_mode=pl.Buffered(1))


def _rms(x):
    return x * lax.rsqrt(jnp.mean(x * x, axis=-1, keepdims=True) + EPS)


def _norm_mod(x, g, shift, scale):
    return (_rms(x) * g) * (1.0 + scale) + shift


def _pair_heads(o_even, o_odd):
    lane = lax.broadcasted_iota(jnp.int32, o_even.shape, 1)
    return jnp.where(lane < HEAD_V, o_even, pltpu.roll(o_odd, HEAD_V, 1))


def _ada_kernel(c_ref, w_ref, b_ref, o_ref):
    c = c_ref[...]
    act = c / (1.0 + jnp.exp(-c))
    o_ref[0] = jnp.dot(act.astype(BF16), w_ref[0], preferred_element_type=F32) + b_ref[0]


def _ada_mods(c_rows, ada_w, ada_b):
    depth, d, n = ada_w.shape
    tn = 1536
    return pl.pallas_call(
        _ada_kernel,
        grid=(depth, n // tn),
        in_specs=[
            pl.BlockSpec((c_rows.shape[0], d), lambda i, j: (0, 0)),
            pl.BlockSpec((1, d, tn), lambda i, j: (i, 0, j)),
            pl.BlockSpec((1, 1, tn), lambda i, j: (i, 0, j)),
        ],
        out_specs=pl.BlockSpec((1, c_rows.shape[0], tn), lambda i, j: (i, 0, j)),
        out_shape=jax.ShapeDtypeStruct((depth, c_rows.shape[0], n), F32),
        compiler_params=_cparams(2),
        name="ada_mods",
    )(c_rows, ada_w.astype(BF16), ada_b.reshape(depth, 1, n))


class _Geom:
    def __init__(self, b, n, l):
        assert n % l == 0
        self.b, self.n, self.l = b, n, l
        self.tm = l
        self.tpb = (n + l) // l
        self.xtpb = n // l
        self.tiles = b * self.tpb
        self.rows = self.tiles * self.tm

    def mod_row(self, t):
        return jnp.where(t % self.tpb == 0, self.b, t // self.tpb)

    def pos_block(self, t):
        return t % self.tpb


def _mod_spec(geom, layer, tile_of):
    return pl.BlockSpec((None, None, 1, N_MOD * D_MODEL),
                        lambda t: (layer, geom.mod_row(tile_of(t)), 0, 0))


def _row_spec(geom, width):
    return pl.BlockSpec((geom.tm, width), lambda t: (t, 0))


def _pos_spec(geom):
    return pl.BlockSpec((geom.tm, LANES), lambda t: (geom.pos_block(t), 0))


def _pos_spec_t(geom):
    return pl.BlockSpec((LANES, geom.tm), lambda t: (0, geom.pos_block(t)))


def _rope_cos_sin(n, rot_dim):
    t = jnp.arange(n, dtype=jnp.int32)
    row = (t // GRID_W).astype(F32)
    col = (t % GRID_W).astype(F32)
    n_freq = rot_dim // 4
    inv_freq = ROPE_BASE ** (-jnp.arange(n_freq, dtype=F32) / n_freq)
    ang = jnp.concatenate([row[:, None] * inv_freq, col[:, None] * inv_freq], axis=-1)
    return jnp.cos(ang), jnp.sin(ang)


def _rope_tables(geom, rot_dim, lead, scale):
    c, s = _rope_cos_sin(geom.n, rot_dim)
    half = rot_dim // 2
    pad = LANES - lead - rot_dim
    n, l = geom.n, geom.l
    cos_t = jnp.concatenate([jnp.ones((n, lead), F32), c, c, jnp.zeros((n, pad), F32)], axis=1)
    sin_t = jnp.concatenate([jnp.zeros((n, lead), F32), -s, s, jnp.zeros((n, pad), F32)], axis=1)
    cos_c = jnp.concatenate([jnp.ones((l, lead + 2 * half), F32), jnp.zeros((l, pad), F32)], axis=1)
    sin_c = jnp.zeros((l, LANES), F32)
    return (jnp.concatenate([cos_c, cos_t], axis=0) * scale,
            jnp.concatenate([sin_c, sin_t], axis=0) * scale)


def _ones_slot_vector(n_slots):
    row = np.zeros((n_slots, LANES), np.float32)
    row[:, HEAD_V] = 1.0
    return row.reshape(n_slots * LANES)


def _prep_mla_weights(w_in, w_qb, w_kvb):
    d = w_in.shape[0]
    pe0 = MLA_Q_RANK + MLA_KV_RANK
    ev = pe0 + 2 * np.arange(MLA_ROPE // 2)
    od = ev + 1
    z = jnp.zeros((d, LANES - MLA_ROPE), F32)
    win = jnp.concatenate([w_in[:, :pe0], w_in[:, ev], w_in[:, od], z, w_in[:, od], w_in[:, ev], z], axis=1)

    hd = MLA_NOPE + MLA_ROPE
    wq3 = w_qb.reshape(MLA_Q_RANK, MLA_HEADS, hd)
    nope, pe_e, pe_o = wq3[:, :, :MLA_NOPE], wq3[:, :, MLA_NOPE::2], wq3[:, :, MLA_NOPE + 1::2]
    zpad = jnp.zeros((MLA_Q_RANK, MLA_HEADS, LANES - hd), F32)
    zlead = jnp.zeros((MLA_Q_RANK, MLA_HEADS, MLA_NOPE), F32)
    wqa = jnp.concatenate([nope, pe_e, pe_o, zpad], axis=-1).reshape(MLA_Q_RANK, MLA_HEADS * LANES)
    wqb = jnp.concatenate([zlead, pe_o, pe_e, zpad], axis=-1).reshape(MLA_Q_RANK, MLA_HEADS * LANES)
    wq_t = jnp.concatenate([wqa, wqb], axis=1).T

    wkv3 = w_kvb.reshape(MLA_KV_RANK, MLA_HEADS, MLA_NOPE + MLA_V)
    zk = jnp.zeros((MLA_KV_RANK, MLA_HEADS, LANES - MLA_NOPE), F32)
    wk_top = jnp.concatenate([wkv3[:, :, :MLA_NOPE], zk], axis=-1).reshape(MLA_KV_RANK, MLA_HEADS * LANES)
    eye = np.zeros((LANES, MLA_HEADS, LANES), np.float32)
    for j in range(MLA_ROPE):
        eye[j, :, MLA_NOPE + j] = 1.0
    wk = jnp.concatenate([wk_top, jnp.asarray(eye.reshape(LANES, MLA_HEADS * LANES))], axis=0)
    zv = jnp.zeros((MLA_KV_RANK, MLA_HEADS, LANES - MLA_V), F32)
    wv_t = jnp.concatenate([wkv3[:, :, MLA_NOPE:], zv], axis=-1).reshape(MLA_KV_RANK, MLA_HEADS * LANES).T
    return win.astype(BF16), wq_t.astype(BF16), wk.astype(BF16), wv_t.astype(BF16)


def _mla_pre_kernel(x_ref, mod_ref, g_ref, win_ref, qag_ref, kvag_ref, wqt_ref, wk_ref, wvt_ref,
                    cqt_ref, sqt_ref, ck_ref, sk_ref, ones_ref, qt_out, k_out, vt_out):
    nt = (((1,), (1,)), ((), ()))
    mod = mod_ref[...]
    h = _norm_mod(x_ref[...], g_ref[...], mod[:, :D_MODEL], mod[:, D_MODEL:2 * D_MODEL]).astype(BF16)
    proj = jnp.dot(h, win_ref[...], preferred_element_type=F32)
    kv0 = MLA_Q_RANK
    pe0 = kv0 + MLA_KV_RANK
    qn = (_rms(proj[:, :kv0]) * qag_ref[...]).astype(BF16)
    kvn = _rms(proj[:, kv0:pe0]) * kvag_ref[...]
    qab_t = lax.dot_general(wqt_ref[...], qn, nt, preferred_element_type=F32)
    cqt, sqt = cqt_ref[...], sqt_ref[...]
    half = MLA_HEADS * LANES
    for hh in range(MLA_HEADS):
        lo = hh * LANES
        qt_out[0, lo:lo + LANES, :] = (qab_t[lo:lo + LANES] * cqt
                                       + qab_t[half + lo:half + lo + LANES] * sqt).astype(BF16)
    kpe = proj[:, pe0:pe0 + LANES] * ck_ref[...] + proj[:, pe0 + LANES:pe0 + 2 * LANES] * sk_ref[...]
    lat = jnp.concatenate([kvn, kpe], axis=1).astype(BF16)
    k_out[...] = jnp.dot(lat, wk_ref[...], preferred_element_type=F32).astype(BF16)
    vt = lax.dot_general(wvt_ref[...], kvn.astype(BF16), nt, preferred_element_type=F32)
    vt_out[0] = (vt + ones_ref[...]).astype(BF16)


def _mla_pre(geom, layer, xall, mods, g, w_in, qa_g, w_qb, kva_g, w_kvb):
    win, wq_t, wk, wv_t = _prep_mla_weights(w_in, w_qb, w_kvb)
    scale = (MLA_NOPE + MLA_ROPE) ** -0.5 * LOG2E
    cq, sq = _rope_tables(geom, MLA_ROPE, MLA_NOPE, scale)
    ck, sk = _rope_tables(geom, MLA_ROPE, 0, 1.0)
    width = MLA_HEADS * LANES
    tm = geom.tm
    t_spec = pl.BlockSpec((1, width, tm), lambda t: (t, 0, 0))
    t_shape = jax.ShapeDtypeStruct((geom.tiles, width, tm), BF16)
    return pl.pallas_call(
        _mla_pre_kernel,
        grid=(geom.tiles,),
        in_specs=[
            _row_spec(geom, D_MODEL), _mod_spec(geom, layer, lambda t: t), _const_spec((1, D_MODEL)),
            _const_spec(win.shape), _const_spec((1, MLA_Q_RANK)), _const_spec((1, MLA_KV_RANK)),
            _const_spec(wq_t.shape), _const_spec(wk.shape), _const_spec(wv_t.shape),
            _pos_spec_t(geom), _pos_spec_t(geom), _pos_spec(geom), _pos_spec(geom),
            _const_spec((width, 1)),
        ],
        out_specs=[t_spec, _row_spec(geom, width), t_spec],
        out_shape=[t_shape, jax.ShapeDtypeStruct((geom.rows, width), BF16), t_shape],
        compiler_params=_cparams(1),
        name="mla_pre",
    )(xall, mods, g.reshape(1, D_MODEL), win, qa_g.reshape(1, -1), kva_g.reshape(1, -1), wq_t, wk, wv_t,
      cq.T, sq.T, ck, sk, jnp.asarray(_ones_slot_vector(MLA_HEADS).reshape(width, 1)))


def _mla_attn_kernel(qt_ref, k_ref, vt_ref, ot_ref, s_a, s_b, *, cs, n_chunks, t0):
    tm = vt_ref.shape[2]
    tk = cs * tm
    t = pl.program_id(2) + t0

    def finish(accs):
        outs = [acc[:HEAD_V] * (1.0 / acc[HEAD_V:HEAD_V + 1]) for acc in accs]
        ot_ref[0] = jnp.concatenate(outs, axis=0).astype(BF16)

    def _context_tile():
        accs = []
        for j in range(2):
            s = jnp.dot(k_ref[0:tm, j * LANES:(j + 1) * LANES], qt_ref[0, j * LANES:(j + 1) * LANES, :],
                        preferred_element_type=F32)
            p = jnp.exp2(s - jnp.max(s, axis=0, keepdims=True)).astype(BF16)
            accs.append(jnp.dot(vt_ref[0, j * LANES:j * LANES + V_ROWS, :], p, preferred_element_type=F32))
        finish(accs)

    if t0 == 0:
        pl.when(t == 0)(_context_tile)

    @pl.when(t > 0)
    def _latent_tile():
        qts = [qt_ref[0, j * LANES:(j + 1) * LANES, :] for j in range(2)]

        def scores(c, s_ref):
            off = pl.multiple_of(c * tk, tm)
            maxes = []
            for j in range(2):
                s = jnp.dot(k_ref[pl.ds(off, tk), j * LANES:(j + 1) * LANES], qts[j],
                            preferred_element_type=F32)
                s_ref[j] = s
                maxes.append(jnp.max(s, axis=0, keepdims=True))
            return tuple(maxes)

        def accumulate(c, s_ref, maxes, ms, accs):
            new_ms, new_accs = [], []
            for j in range(2):
                m_new = jnp.maximum(ms[j], maxes[j])
                alpha = jnp.exp2(ms[j] - m_new)
                p = jnp.exp2(s_ref[j] - m_new).astype(BF16)
                v = jnp.concatenate([vt_ref[c * cs + u, j * LANES:j * LANES + V_ROWS, :] for u in range(cs)],
                                    axis=1)
                new_accs.append(alpha * accs[j] + jnp.dot(v, p, preferred_element_type=F32))
                new_ms.append(m_new)
            return tuple(new_ms), tuple(new_accs)

        tq = qt_ref.shape[2]
        ms = tuple(jnp.full((1, tq), NEG_INF, F32) for _ in range(2))
        accs = tuple(jnp.zeros((V_ROWS, tq), F32) for _ in range(2))
        max_a = scores(0, s_a)
        pairs = (n_chunks - 1) // 2

        def body(i, carry):
            max_a, ms, accs = carry
            max_b = scores(2 * i + 1, s_b)
            ms, accs = accumulate(2 * i, s_a, max_a, ms, accs)
            max_a = scores(2 * i + 2, s_a)
            ms, accs = accumulate(2 * i + 1, s_b, max_b, ms, accs)
            return max_a, ms, accs

        max_a, ms, accs = lax.fori_loop(0, pairs, body, (max_a, ms, accs))
        if n_chunks - 2 * pairs == 2:
            max_b = scores(2 * pairs + 1, s_b)
            ms, accs = accumulate(2 * pairs, s_a, max_a, ms, accs)
            ms, accs = accumulate(2 * pairs + 1, s_b, max_b, ms, accs)
        else:
            ms, accs = accumulate(2 * pairs, s_a, max_a, ms, accs)
        finish(accs)


def _mla_attn(geom, qt, k, vt, need_ctx):
    b, tpb, tm = geom.b, geom.tpb, geom.tm
    cs = 3 if tpb % 3 == 0 else (2 if tpb % 2 == 0 else 1)
    t0 = 0 if need_ctx else 1
    nt = tpb - t0

    def q_map(bi, hp, t):
        return (bi * tpb + t + t0, hp, 0)

    return pl.pallas_call(
        functools.partial(_mla_attn_kernel, cs=cs, n_chunks=tpb // cs, t0=t0),
        grid=(b, MLA_HEADS // 2, nt),
        in_specs=[pl.BlockSpec((1, 2 * LANES, tm), q_map),
                  pl.BlockSpec((tpb * tm, 2 * LANES), lambda bi, hp, t: (bi, hp)),
                  pl.BlockSpec((tpb, 2 * LANES, tm), lambda bi, hp, t: (bi, hp, 0))],
        out_specs=pl.BlockSpec((1, 2 * HEAD_V, tm), q_map),
        out_shape=jax.ShapeDtypeStruct((geom.tiles, MLA_HEADS * HEAD_V, tm), BF16),
        scratch_shapes=[pltpu.VMEM((2, cs * tm, tm), F32), pltpu.VMEM((2, cs * tm, tm), F32)],
        compiler_params=_cparams(3),
        name="mla_attn",
    )(qt, k, vt)


def _prep_swa_weights(w_qkv):
    d = w_qkv.shape[0]
    dq = SWA_Q_HEADS * SWA_HEAD_DIM
    dkv = SWA_KV_HEADS * SWA_HEAD_DIM

    def rope_pair(w, heads):
        w3 = w.reshape(d, heads, SWA_HEAD_DIM)
        e, o = w3[:, :, 0::2], w3[:, :, 1::2]
        z = jnp.zeros((d, heads, LANES - SWA_HEAD_DIM), F32)
        return (jnp.concatenate([e, o, z], axis=-1).reshape(d, heads * LANES),
                jnp.concatenate([o, e, z], axis=-1).reshape(d, heads * LANES))

    wqa, wqb = rope_pair(w_qkv[:, :dq], SWA_Q_HEADS)
    wka, wkb = rope_pair(w_qkv[:, dq:dq + dkv], SWA_KV_HEADS)
    v3 = w_qkv[:, dq + dkv:].reshape(d, SWA_KV_HEADS, SWA_HEAD_DIM)
    wv = jnp.concatenate([v3, jnp.zeros((d, SWA_KV_HEADS, LANES - SWA_HEAD_DIM), F32)], axis=-1)
    wv = wv.reshape(d, SWA_KV_HEADS * LANES)
    return jnp.concatenate([wqa, wqb, wka, wkb, wv], axis=1).astype(BF16)


def _swa_pre_kernel(x_ref, mod_ref, g_ref, w_ref, cq_ref, sq_ref, ck_ref, sk_ref, ones_ref,
                    q_out, k_out, v_out):
    mod = mod_ref[...]
    h = _norm_mod(x_ref[...], g_ref[...], mod[:, :D_MODEL], mod[:, D_MODEL:2 * D_MODEL]).astype(BF16)
    qw = SWA_Q_HEADS * LANES
    kw = SWA_KV_HEADS * LANES
    qab = jnp.dot(h, w_ref[:, :2 * qw], preferred_element_type=F32)
    cq, sq = cq_ref[...], sq_ref[...]
    for hh in range(SWA_Q_HEADS):
        lo = hh * LANES
        q_out[:, lo:lo + LANES] = (qab[:, lo:lo + LANES] * cq
                                   + qab[:, qw + lo:qw + lo + LANES] * sq).astype(BF16)
    kv = jnp.dot(h, w_ref[:, 2 * qw:], preferred_element_type=F32)
    ck, sk = ck_ref[...], sk_ref[...]
    for hh in range(SWA_KV_HEADS):
        lo = hh * LANES
        k_out[:, lo:lo + LANES] = (kv[:, lo:lo + LANES] * ck
                                   + kv[:, kw + lo:kw + lo + LANES] * sk).astype(BF16)
    v_out[...] = (kv[:, 2 * kw:] + ones_ref[...]).astype(BF16)


def _swa_pre(geom, layer, xall, mods, g, w_qkv):
    w = _prep_swa_weights(w_qkv)
    scale = SWA_HEAD_DIM ** -0.5
    cq, sq = _rope_tables(geom, SWA_HEAD_DIM, 0, scale)
    ck, sk = _rope_tables(geom, SWA_HEAD_DIM, 0, 1.0)
    qw, kw = SWA_Q_HEADS * LANES, SWA_KV_HEADS * LANES
    return pl.pallas_call(
        _swa_pre_kernel,
        grid=(geom.tiles,),
        in_specs=[
            _row_spec(geom, D_MODEL), _mod_spec(geom, layer, lambda t: t), _const_spec((1, D_MODEL)),
            _const_spec(w.shape),
            _pos_spec(geom), _pos_spec(geom), _pos_spec(geom), _pos_spec(geom),
            _const_spec((1, kw)),
        ],
        out_specs=[_row_spec(geom, qw), _row_spec(geom, kw), _row_spec(geom, kw)],
        out_shape=[jax.ShapeDtypeStruct((geom.rows, qw), BF16),
                   jax.ShapeDtypeStruct((geom.rows, kw), BF16),
                   jax.ShapeDtypeStruct((geom.rows, kw), BF16)],
        compiler_params=_cparams(1),
        name="swa_pre",
    )(xall, mods, g.reshape(1, D_MODEL), w, cq, sq, ck, sk,
      jnp.asarray(_ones_slot_vector(SWA_KV_HEADS).reshape(1, kw)))


def _prep_na_weights(w_qkv):
    d = w_qkv.shape[0]
    dm = NA_HEADS * NA_HEAD_DIM
    z = jnp.zeros((d, NA_HEADS, LANES - NA_HEAD_DIM), F32)

    def slots(w):
        return jnp.concatenate([w.reshape(d, NA_HEADS, NA_HEAD_DIM), z], axis=-1).reshape(d, NA_HEADS * LANES)

    scale = NA_HEAD_DIM ** -0.5
    return jnp.concatenate([slots(w_qkv[:, :dm]) * scale, slots(w_qkv[:, dm:2 * dm]),
                            slots(w_qkv[:, 2 * dm:])], axis=1).astype(BF16)


def _na_pre_kernel(x_ref, mod_ref, g_ref, w_ref, ones_ref, q_out, k_out, v_out):
    mod = mod_ref[...]
    h = _norm_mod(x_ref[...], g_ref[...], mod[:, :D_MODEL], mod[:, D_MODEL:2 * D_MODEL]).astype(BF16)
    hw = NA_HEADS * LANES
    q_out[...] = jnp.dot(h, w_ref[:, :hw], preferred_element_type=F32).astype(BF16)
    k_out[...] = jnp.dot(h, w_ref[:, hw:2 * hw], preferred_element_type=F32).astype(BF16)
    v_out[...] = (jnp.dot(h, w_ref[:, 2 * hw:], preferred_element_type=F32) + ones_ref[...]).astype(BF16)


def _na_pre(geom, layer, xall, mods, g, w_qkv):
    w = _prep_na_weights(w_qkv)
    hw = NA_HEADS * LANES
    out = jax.ShapeDtypeStruct((geom.rows, hw), BF16)
    return pl.pallas_call(
        _na_pre_kernel,
        grid=(geom.tiles,),
        in_specs=[_row_spec(geom, D_MODEL), _mod_spec(geom, layer, lambda t: t), _const_spec((1, D_MODEL)),
                  _const_spec(w.shape), _const_spec((1, hw))],
        out_specs=[_row_spec(geom, hw)] * 3,
        out_shape=[out, out, out],
        compiler_params=_cparams(1),
        name="na_pre",
    )(xall, mods, g.reshape(1, D_MODEL), w, jnp.asarray(_ones_slot_vector(NA_HEADS).reshape(1, hw)))


def _band_kernel(*refs, tq, hbq, grp, has_sink):
    if has_sink:
        q_ref, kp, kc, kn, kx, vp, vc, vn, vx, bias_ref, sink_ref, o_ref = refs
    else:
        q_ref, kp, kc, kn, kx, vp, vc, vn, vx, bias_ref, o_ref = refs
    hbk = hbq // grp
    shared_bias = bias_ref.shape[1] == 1
    dn = (((1,), (1,)), ((), ()))
    head0 = pl.program_id(1) * hbq
    outs = []
    for j in range(hbk):
        ls = slice(j * LANES, (j + 1) * LANES)
        kcat = jnp.concatenate([kp[:, ls], kc[:, ls], kn[:, ls], kx[:, ls]], axis=0)
        vcat = jnp.concatenate([vp[:, ls], vc[:, ls], vn[:, ls], vx[:, ls]], axis=0)
        width = kcat.shape[0]
        if grp > 1:
            qs = jnp.concatenate([q_ref[:, (j * grp + i) * LANES:(j * grp + i + 1) * LANES]
                                  for i in range(grp)], axis=0)
        else:
            qs = q_ref[:, ls]
        s = lax.dot_general(qs, kcat, dn, preferred_element_type=F32)
        if shared_bias:
            s = (s.reshape(grp, tq, width) + bias_ref[0, 0][None]).reshape(grp * tq, width)
        else:
            s = s + bias_ref[0, j]
        m = jnp.max(s, axis=-1, keepdims=True)
        if has_sink:
            row_head = lax.broadcasted_iota(jnp.int32, (grp * tq, 1), 0) // tq
            sink_col = jnp.zeros((grp * tq, 1), F32)
            for i in range(grp):
                sink_col = jnp.where(row_head == i, sink_ref[head0 + j * grp + i], sink_col)
            m = jnp.maximum(m, sink_col)
        p = jnp.exp(s - m)
        acc = jnp.dot(p.astype(BF16), vcat, preferred_element_type=F32)
        denom = acc[:, HEAD_V:HEAD_V + 1]
        if has_sink:
            denom = denom + jnp.exp(sink_col - m)
        o = acc * (1.0 / denom)
        for i in range(grp):
            outs.append(o[i * tq:(i + 1) * tq])
    for pi in range(hbq // 2):
        o_ref[:, pi * LANES:(pi + 1) * LANES] = _pair_heads(outs[2 * pi], outs[2 * pi + 1]).astype(BF16)


def _band_attn(geom, q, k, v, bias, sink, *, tq, hbq, grp, need_ctx, name):
    b, n, l = geom.b, geom.n, geom.l
    n_q_heads = q.shape[1] // LANES
    hbk = hbq // grp
    n_groups = n_q_heads // hbq
    tiles_x = n // tq
    ctx_tiles = l // tq
    batch_tiles = tiles_x + ctx_tiles
    nt = tiles_x + (ctx_tiles if need_ctx else 0)

    def q_map(bi, g, t):
        return (bi * batch_tiles + jnp.where(t < tiles_x, ctx_tiles + t, t - tiles_x), g)

    def band_map(delta):
        def index(bi, g, t):
            tt = jnp.clip(jnp.minimum(t, tiles_x - 1) + delta, 0, tiles_x - 1)
            return (bi * batch_tiles + ctx_tiles + tt, g)
        return index

    def bias_map(bi, g, t):
        variant = jnp.where(t >= tiles_x, 3, jnp.where(t == 0, 0, jnp.where(t == tiles_x - 1, 2, 1)))
        return (variant, g if bias.shape[1] > 1 else 0, 0, 0)

    band_specs = [pl.BlockSpec((tq, hbk * LANES), band_map(dl)) for dl in (-1, 0, 1)]
    ctx_spec = pl.BlockSpec((l, hbk * LANES), lambda bi, g, t: (bi * geom.tpb, g))
    hb_bias = hbq if bias.shape[1] > 1 else 1
    in_specs = ([pl.BlockSpec((tq, hbq * LANES), q_map)] + band_specs + [ctx_spec] + band_specs + [ctx_spec]
                + [pl.BlockSpec((1, hb_bias, tq, 3 * tq + l), bias_map)])
    args = [q, k, k, k, k, v, v, v, v, bias]
    if sink is not None:
        in_specs.append(pl.BlockSpec(memory_space=pltpu.SMEM))
        args.append(sink)
    return pl.pallas_call(
        functools.partial(_band_kernel, tq=tq, hbq=hbq, grp=grp, has_sink=sink is not None),
        grid=(b, n_groups, nt),
        in_specs=in_specs,
        out_specs=pl.BlockSpec((tq, hbq * HEAD_V), q_map),
        out_shape=jax.ShapeDtypeStruct((geom.rows, n_q_heads * HEAD_V), BF16),
        compiler_params=_cparams(3),
        name=name,
    )(*args)


def _swa_bias(l):
    tq = SWA_BLOCK
    i = np.arange(tq)[:, None]
    j = np.arange(tq)[None, :]
    prev_ok = (SWA_BLOCK + i - j) <= SWA_WINDOW
    next_ok = (SWA_BLOCK + j - i) <= SWA_WINDOW
    cur_ok = np.abs(i - j) <= SWA_WINDOW
    none = np.zeros((tq, tq), bool)
    ctx_ok = np.ones((tq, l), bool)
    variants = [
        np.concatenate([none, cur_ok, next_ok, ctx_ok], axis=1),
        np.concatenate([prev_ok, cur_ok, next_ok, ctx_ok], axis=1),
        np.concatenate([prev_ok, cur_ok, none, ctx_ok], axis=1),
        np.concatenate([none, none, none, ctx_ok], axis=1),
    ]
    ok = np.stack(variants)[:, None]
    return jnp.asarray(np.where(ok, 0.0, NEG_INF).astype(np.float32))


NA_TILE_ROWS = 4


def _na_bias(rel_bias, l):
    col = np.arange(GRID_W)
    col_start = np.clip(col - NA_COLS // 2, 0, GRID_W - NA_COLS)
    col_valid = (col[None, :] >= col_start[:, None]) & (col[None, :] < col_start[:, None] + NA_COLS)
    dc_idx = np.clip(col[None, :] - col[:, None] + NA_COLS - 1, 0, 2 * NA_COLS - 2)
    bias_cols = rel_bias.astype(F32)[:, :, dc_idx]
    bias_cols = jnp.where(col_valid, bias_cols, NEG_INF)
    r = NA_TILE_ROWS
    a = np.arange(r)[:, None]
    c = np.arange(3 * r)[None, :]
    dr_idx = np.clip(c - r - a + NA_ROWS - 1, 0, 2 * NA_ROWS - 2)
    tiles = bias_cols[:, dr_idx]
    tiles = tiles.transpose(0, 1, 3, 2, 4)
    half = NA_ROWS // 2
    interior = (c - r >= a - half) & (c - r < a - half + NA_ROWS)
    first = (c >= r) & (c < r + NA_ROWS) & (a >= 0)
    last = (c >= 2 * r - NA_ROWS) & (c < 2 * r) & (a >= 0)
    none = np.zeros_like(interior)
    h = rel_bias.shape[0]
    out = []
    for ok in (first, interior, last, none):
        ok5 = ok[None, :, None, :, None]
        t = jnp.where(ok5, tiles, NEG_INF).reshape(h, r * GRID_W, 3 * r * GRID_W)
        out.append(jnp.concatenate([t, jnp.zeros((h, r * GRID_W, l), F32)], axis=-1))
    return jnp.stack(out)


def _post_kernel(*refs, final, o_transposed):
    if final:
        x_ref, o_ref, mod_ref, wo_ref, g_ref, w1_ref, w2_ref, gout_ref, out_ref = refs
    else:
        x_ref, o_ref, mod_ref, wo_ref, g_ref, w1_ref, w2_ref, out_ref = refs
    d = D_MODEL
    mod = mod_ref[...]
    g1, sh2, sc2, g2 = mod[:, 2 * d:3 * d], mod[:, 3 * d:4 * d], mod[:, 4 * d:5 * d], mod[:, 5 * d:6 * d]
    if o_transposed:
        y = lax.dot_general(o_ref[0], wo_ref[...], (((0,), (0,)), ((), ())), preferred_element_type=F32)
    else:
        y = jnp.dot(o_ref[...], wo_ref[...], preferred_element_type=F32)
    x1 = x_ref[...] + g1 * y
    h2 = _norm_mod(x1, g_ref[...], sh2, sc2).astype(BF16)
    acc = jnp.zeros(x1.shape, F32)
    for c in range(D_FF // d):
        u = jnp.maximum(jnp.dot(h2, w1_ref[:, c * d:(c + 1) * d], preferred_element_type=F32), 0.0)
        acc = acc + jnp.dot((u * u).astype(BF16), w2_ref[c * d:(c + 1) * d, :], preferred_element_type=F32)
    x2 = x1 + g2 * acc
    if final:
        x2 = _rms(x2) * gout_ref[...]
    out_ref[...] = x2


def _post(geom, layer, xall, o, o_transposed, mods, w_o, g, w1, w2, g_out):
    final = g_out is not None
    tm = geom.tm
    if final:
        tiles = geom.b * geom.xtpb
        tile_of = lambda t: (t // geom.xtpb) * geom.tpb + 1 + t % geom.xtpb
    else:
        tiles = geom.tiles
        tile_of = lambda t: t
    x_spec = pl.BlockSpec((tm, D_MODEL), lambda t: (tile_of(t), 0))
    if o_transposed:
        o_spec = pl.BlockSpec((1, D_MODEL, tm), lambda t: (tile_of(t), 0, 0))
    else:
        o_spec = x_spec
    in_specs = [x_spec, o_spec, _mod_spec(geom, layer, tile_of),
                _const_spec(w_o.shape), _const_spec((1, D_MODEL)), _const_spec(w1.shape), _const_spec(w2.shape)]
    args = [xall, o, mods, w_o.astype(BF16), g.reshape(1, D_MODEL), w1.astype(BF16), w2.astype(BF16)]
    if final:
        in_specs.append(_const_spec((1, D_MODEL)))
        args.append(g_out.reshape(1, D_MODEL))
    return pl.pallas_call(
        functools.partial(_post_kernel, final=final, o_transposed=o_transposed),
        grid=(tiles,),
        in_specs=in_specs,
        out_specs=pl.BlockSpec((tm, D_MODEL), lambda t: (t, 0)),
        out_shape=jax.ShapeDtypeStruct((tiles * tm, D_MODEL), F32),
        compiler_params=_cparams(1),
        name="post_mlp",
    )(*args)


def kernel(x, c, ctx, c_ctx, ada_w, ada_b, norm_mix_g, norm_mlp_g, norm_out_g, mlp_w1, mlp_w2,
           mla_w_in, mla_qa_g, mla_w_qb, mla_kva_g, mla_w_kvb, mla_w_o,
           swa_w_qkv, swa_sink, swa_w_o, na_w_qkv, na_rel_bias, na_w_o):
    b, n, d = x.shape
    l = ctx.shape[1]
    assert d == D_MODEL and n % GRID_W == 0 and l % SWA_BLOCK == 0 and l == NA_TILE_ROWS * GRID_W
    geom = _Geom(b, n, l)

    mod_rows = -(-(b + 1) // 8) * 8
    c_rows = jnp.concatenate([c, c_ctx[None, :], jnp.zeros((mod_rows - b - 1, d), F32)], axis=0)
    mods = _ada_mods(c_rows, ada_w, ada_b).reshape(DEPTH, mod_rows, 1, N_MOD * d)

    xall = jnp.concatenate([ctx, x], axis=1).reshape(geom.rows, d)
    for i in range(DEPTH):
        need_ctx = i < DEPTH - 1
        kind, j = i % 3, i // 3
        if kind == 0:
            qt, k, vt = _mla_pre(geom, i, xall, mods, norm_mix_g[i], mla_w_in[j], mla_qa_g[j], mla_w_qb[j],
                                 mla_kva_g[j], mla_w_kvb[j])
            o = _mla_attn(geom, qt, k, vt, need_ctx)
            w_o = mla_w_o[j]
        elif kind == 1:
            q, k, v = _swa_pre(geom, i, xall, mods, norm_mix_g[i], swa_w_qkv[j])
            o = _band_attn(geom, q, k, v, _swa_bias(l), swa_sink[j], tq=SWA_BLOCK, hbq=SWA_Q_HEADS,
                           grp=SWA_Q_HEADS // SWA_KV_HEADS, need_ctx=need_ctx, name="swa_attn")
            w_o = swa_w_o[j]
        else:
            q, k, v = _na_pre(geom, i, xall, mods, norm_mix_g[i], na_w_qkv[j])
            o = _band_attn(geom, q, k, v, _na_bias(na_rel_bias[j], l), None, tq=NA_TILE_ROWS * GRID_W,
                           hbq=4, grp=1, need_ctx=need_ctx, name="na_attn")
            w_o = na_w_o[j]
        g_out = norm_out_g if i == DEPTH - 1 else None
        xall = _post(geom, i, xall, o, kind == 0, mods, w_o, norm_mlp_g[i], mlp_w1[i], mlp_w2[i], g_out)
    return xall.reshape(b, n, d)
```

```python
import functools
import math

import numpy as np
import jax
import jax.numpy as jnp
from jax import lax
from jax.experimental import pallas as pl
from jax.experimental.pallas import tpu as pltpu

F32 = jnp.float32
BF16 = jnp.bfloat16

D_MODEL = 1024
D_FF = 4 * D_MODEL
DEPTH = 4
N_MOD = 6
GRID_W = 64
EPS = 1e-6
ROPE_BASE = 10000.0
NEG_INF = -1e30

MLA_HEADS = 16
MLA_Q_RANK = 256
MLA_KV_RANK = 128
MLA_NOPE = 64
MLA_ROPE = 32
MLA_V = 64

SWA_Q_HEADS = 16
SWA_KV_HEADS = 4
SWA_HEAD_DIM = 64
SWA_WINDOW = 128
SWA_BLOCK = 128

NA_HEADS = 16
NA_HEAD_DIM = 64
NA_ROWS = 8
NA_COLS = 16

LANES = 128
HEAD_V = 64
CHUNK_UNROLL = 4
V_ROWS = 80
VMEM_LIMIT = 56 * 1024 * 1024
LOG2E = math.log2(math.e)


def _cparams(n_axes):
    return pltpu.CompilerParams(dimension_semantics=("arbitrary",) * n_axes,
                                vmem_limit_bytes=VMEM_LIMIT)


def _const_spec(shape):
    nd = len(shape)
    return pl.BlockSpec(shape, lambda *_: (0,) * nd, pipeline_mode=pl.Buffered(1))


def _rms(x):
    return x * lax.rsqrt(jnp.mean(x * x, axis=-1, keepdims=True) + EPS)


def _norm_mod(x, g, shift, scale):
    return (_rms(x) * g) * (1.0 + scale) + shift


def _pair_heads(o_even, o_odd):
    lane = lax.broadcasted_iota(jnp.int32, o_even.shape, 1)
    return jnp.where(lane < HEAD_V, o_even, pltpu.roll(o_odd, HEAD_V, 1))


def _ada_kernel(c_ref, w_ref, b_ref, o_ref):
    c = c_ref[...]
    act = c / (1.0 + jnp.exp(-c))
    o_ref[0] = jnp.dot(act.astype(BF16), w_ref[0], preferred_element_type=F32) + b_ref[0]


def _ada_mods(c_rows, ada_w, ada_b):
    depth, d, n = ada_w.shape
    tn = 1536
    return pl.pallas_call(
        _ada_kernel,
        grid=(depth, n // tn),
        in_specs=[
            pl.BlockSpec((c_rows.shape[0], d), lambda i, j: (0, 0)),
            pl.BlockSpec((1, d, tn), lambda i, j: (i, 0, j)),
            pl.BlockSpec((1, 1, tn), lambda i, j: (i, 0, j)),
        ],
        out_specs=pl.BlockSpec((1, c_rows.shape[0], tn), lambda i, j: (i, 0, j)),
        out_shape=jax.ShapeDtypeStruct((depth, c_rows.shape[0], n), F32),
        compiler_params=_cparams(2),
        name="ada_mods",
    )(c_rows, ada_w.astype(BF16), ada_b.reshape(depth, 1, n))


class _Geom:
    def __init__(self, b, n, l):
        assert n % l == 0
        self.b, self.n, self.l = b, n, l
        self.tm = l
        self.tpb = (n + l) // l
        self.xtpb = n // l
        self.tiles = b * self.tpb
        self.rows = self.tiles * self.tm

    def mod_row(self, t):
        return jnp.where(t % self.tpb == 0, self.b, t // self.tpb)

    def pos_block(self, t):
        return t % self.tpb


def _mod_spec(geom, layer, tile_of):
    return pl.BlockSpec((None, None, 1, N_MOD * D_MODEL),
                        lambda t: (layer, geom.mod_row(tile_of(t)), 0, 0))


def _row_spec(geom, width):
    return pl.BlockSpec((geom.tm, width), lambda t: (t, 0))


def _pos_spec(geom):
    return pl.BlockSpec((geom.tm, LANES), lambda t: (geom.pos_block(t), 0))


def _pos_spec_t(geom):
    return pl.BlockSpec((LANES, geom.tm), lambda t: (0, geom.pos_block(t)))


def _rope_cos_sin(n, rot_dim):
    t = jnp.arange(n, dtype=jnp.int32)
    row = (t // GRID_W).astype(F32)
    col = (t % GRID_W).astype(F32)
    n_freq = rot_dim // 4
    inv_freq = ROPE_BASE ** (-jnp.arange(n_freq, dtype=F32) / n_freq)
    ang = jnp.concatenate([row[:, None] * inv_freq, col[:, None] * inv_freq], axis=-1)
    return jnp.cos(ang), jnp.sin(ang)


def _rope_tables(geom, rot_dim, lead, scale):
    c, s = _rope_cos_sin(geom.n, rot_dim)
    half = rot_dim // 2
    pad = LANES - lead - rot_dim
    n, l = geom.n, geom.l
    cos_t = jnp.concatenate([jnp.ones((n, lead), F32), c, c, jnp.zeros((n, pad), F32)], axis=1)
    sin_t = jnp.concatenate([jnp.zeros((n, lead), F32), -s, s, jnp.zeros((n, pad), F32)], axis=1)
    cos_c = jnp.concatenate([jnp.ones((l, lead + 2 * half), F32), jnp.zeros((l, pad), F32)], axis=1)
    sin_c = jnp.zeros((l, LANES), F32)
    return (jnp.concatenate([cos_c, cos_t], axis=0) * scale,
            jnp.concatenate([sin_c, sin_t], axis=0) * scale)


def _ones_slot_vector(n_slots):
    row = np.zeros((n_slots, LANES), np.float32)
    row[:, HEAD_V] = 1.0
    return row.reshape(n_slots * LANES)


def _prep_mla_weights(w_in, w_qb, w_kvb):
    d = w_in.shape[0]
    pe0 = MLA_Q_RANK + MLA_KV_RANK
    ev = pe0 + 2 * np.arange(MLA_ROPE // 2)
    od = ev + 1
    z = jnp.zeros((d, LANES - MLA_ROPE), F32)
    win = jnp.concatenate([w_in[:, :pe0], w_in[:, ev], w_in[:, od], z, w_in[:, od], w_in[:, ev], z], axis=1)

    hd = MLA_NOPE + MLA_ROPE
    wq3 = w_qb.reshape(MLA_Q_RANK, MLA_HEADS, hd)
    nope, pe_e, pe_o = wq3[:, :, :MLA_NOPE], wq3[:, :, MLA_NOPE::2], wq3[:, :, MLA_NOPE + 1::2]
    zpad = jnp.zeros((MLA_Q_RANK, MLA_HEADS, LANES - hd), F32)
    zlead = jnp.zeros((MLA_Q_RANK, MLA_HEADS, MLA_NOPE), F32)
    wqa = jnp.concatenate([nope, pe_e, pe_o, zpad], axis=-1).reshape(MLA_Q_RANK, MLA_HEADS * LANES)
    wqb = jnp.concatenate([zlead, pe_o, pe_e, zpad], axis=-1).reshape(MLA_Q_RANK, MLA_HEADS * LANES)
    wq_t = jnp.concatenate([wqa, wqb], axis=1).T

    wkv3 = w_kvb.reshape(MLA_KV_RANK, MLA_HEADS, MLA_NOPE + MLA_V)
    zk = jnp.zeros((MLA_KV_RANK, MLA_HEADS, LANES - MLA_NOPE), F32)
    wk_top = jnp.concatenate([wkv3[:, :, :MLA_NOPE], zk], axis=-1).reshape(MLA_KV_RANK, MLA_HEADS * LANES)
    eye = np.zeros((LANES, MLA_HEADS, LANES), np.float32)
    for j in range(MLA_ROPE):
        eye[j, :, MLA_NOPE + j] = 1.0
    wk = jnp.concatenate([wk_top, jnp.asarray(eye.reshape(LANES, MLA_HEADS * LANES))], axis=0)
    zv = jnp.zeros((MLA_KV_RANK, MLA_HEADS, LANES - MLA_V), F32)
    wv_t = jnp.concatenate([wkv3[:, :, MLA_NOPE:], zv], axis=-1).reshape(MLA_KV_RANK, MLA_HEADS * LANES).T
    return win.astype(BF16), wq_t.astype(BF16), wk.astype(BF16), wv_t.astype(BF16)


def _mla_pre_kernel(x_ref, mod_ref, g_ref, win_ref, qag_ref, kvag_ref, wqt_ref, wk_ref, wvt_ref,
                    cqt_ref, sqt_ref, ck_ref, sk_ref, ones_ref, qt_out, k_out, vt_out):
    nt = (((1,), (1,)), ((), ()))
    mod = mod_ref[...]
    h = _norm_mod(x_ref[...], g_ref[...], mod[:, :D_MODEL], mod[:, D_MODEL:2 * D_MODEL]).astype(BF16)
    proj = jnp.dot(h, win_ref[...], preferred_element_type=F32)
    kv0 = MLA_Q_RANK
    pe0 = kv0 + MLA_KV_RANK
    qn = (_rms(proj[:, :kv0]) * qag_ref[...]).astype(BF16)
    kvn = _rms(proj[:, kv0:pe0]) * kvag_ref[...]
    qab_t = lax.dot_general(wqt_ref[...], qn, nt, preferred_element_type=F32)
    cqt, sqt = cqt_ref[...], sqt_ref[...]
    half = MLA_HEADS * LANES
    for hh in range(MLA_HEADS):
        lo = hh * LANES
        qt_out[0, lo:lo + LANES, :] = (qab_t[lo:lo + LANES] * cqt
                                       + qab_t[half + lo:half + lo + LANES] * sqt).astype(BF16)
    kpe = proj[:, pe0:pe0 + LANES] * ck_ref[...] + proj[:, pe0 + LANES:pe0 + 2 * LANES] * sk_ref[...]
    lat = jnp.concatenate([kvn, kpe], axis=1).astype(BF16)
    k_out[...] = jnp.dot(lat, wk_ref[...], preferred_element_type=F32).astype(BF16)
    vt = lax.dot_general(wvt_ref[...], kvn.astype(BF16), nt, preferred_element_type=F32)
    vt_out[0] = (vt + ones_ref[...]).astype(BF16)


def _mla_pre(geom, layer, xall, mods, g, w_in, qa_g, w_qb, kva_g, w_kvb):
    win, wq_t, wk, wv_t = _prep_mla_weights(w_in, w_qb, w_kvb)
    scale = (MLA_NOPE + MLA_ROPE) ** -0.5 * LOG2E
    cq, sq = _rope_tables(geom, MLA_ROPE, MLA_NOPE, scale)
    ck, sk = _rope_tables(geom, MLA_ROPE, 0, 1.0)
    width = MLA_HEADS * LANES
    tm = geom.tm
    t_spec = pl.BlockSpec((1, width, tm), lambda t: (t, 0, 0))
    t_shape = jax.ShapeDtypeStruct((geom.tiles, width, tm), BF16)
    return pl.pallas_call(
        _mla_pre_kernel,
        grid=(geom.tiles,),
        in_specs=[
            _row_spec(geom, D_MODEL), _mod_spec(geom, layer, lambda t: t), _const_spec((1, D_MODEL)),
            _const_spec(win.shape), _const_spec((1, MLA_Q_RANK)), _const_spec((1, MLA_KV_RANK)),
            _const_spec(wq_t.shape), _const_spec(wk.shape), _const_spec(wv_t.shape),
            _pos_spec_t(geom), _pos_spec_t(geom), _pos_spec(geom), _pos_spec(geom),
            _const_spec((width, 1)),
        ],
        out_specs=[t_spec, _row_spec(geom, width), t_spec],
        out_shape=[t_shape, jax.ShapeDtypeStruct((geom.rows, width), BF16), t_shape],
        compiler_params=_cparams(1),
        name="mla_pre",
    )(xall, mods, g.reshape(1, D_MODEL), win, qa_g.reshape(1, -1), kva_g.reshape(1, -1), wq_t, wk, wv_t,
      cq.T, sq.T, ck, sk, jnp.asarray(_ones_slot_vector(MLA_HEADS).reshape(width, 1)))


def _mla_attn_kernel(qt_ref, k_ref, vt_ref, ot_ref, s_a, s_b, *, cs, n_chunks, t0):
    tm = vt_ref.shape[2]
    tk = cs * tm
    t = pl.program_id(2) + t0

    def finish(accs):
        outs = [acc[:HEAD_V] * (1.0 / acc[HEAD_V:HEAD_V + 1]) for acc in accs]
        ot_ref[0] = jnp.concatenate(outs, axis=0).astype(BF16)

    def _context_tile():
        accs = []
        for j in range(2):
            s = jnp.dot(k_ref[0:tm, j * LANES:(j + 1) * LANES], qt_ref[0, j * LANES:(j + 1) * LANES, :],
                        preferred_element_type=F32)
            p = jnp.exp2(s - jnp.max(s, axis=0, keepdims=True)).astype(BF16)
            accs.append(jnp.dot(vt_ref[0, j * LANES:j * LANES + V_ROWS, :], p, preferred_element_type=F32))
        finish(accs)

    if t0 == 0:
        pl.when(t == 0)(_context_tile)

    @pl.when(t > 0)
    def _latent_tile():
        qts = [qt_ref[0, j * LANES:(j + 1) * LANES, :] for j in range(2)]

        tq = qt_ref.shape[2]

        def qk(c, j, u, s_ref):
            r0 = pl.multiple_of((c * cs + u) * tm, tm)
            s = jnp.dot(k_ref[pl.ds(r0, tm), j * LANES:(j + 1) * LANES], qts[j], preferred_element_type=F32)
            s_ref[j, u * tm:(u + 1) * tm, :] = s
            return jnp.max(s.reshape(tm // 8, 8, tq), axis=0)

        def pv(c, j, u, s_ref, m_new):
            p = jnp.exp2(s_ref[j, u * tm:(u + 1) * tm, :] - m_new).astype(BF16)
            return jnp.dot(vt_ref[c * cs + u, j * LANES:j * LANES + V_ROWS, :], p, preferred_element_type=F32)

        def first_scores(s_ref):
            maxes = []
            for j in range(2):
                part = qk(0, j, 0, s_ref)
                for u in range(1, cs):
                    part = jnp.maximum(part, qk(0, j, u, s_ref))
                maxes.append(jnp.max(part, axis=0, keepdims=True))
            return tuple(maxes)

        def step(c, s_cur, s_next, maxes, ms, accs, with_next):
            new_ms, new_accs, next_maxes = [], [], []
            for j in range(2):
                m_new = jnp.maximum(ms[j], maxes[j])
                alpha = jnp.exp2(ms[j] - m_new)
                part = qk(c + 1, j, 0, s_next) if with_next else None
                acc_new = None
                for u in range(cs):
                    if with_next and u + 1 < cs:
                        part = jnp.maximum(part, qk(c + 1, j, u + 1, s_next))
                    d = pv(c, j, u, s_cur, m_new)
                    acc_new = d if acc_new is None else acc_new + d
                new_accs.append(alpha * accs[j] + acc_new)
                new_ms.append(m_new)
                if with_next:
                    next_maxes.append(jnp.max(part, axis=0, keepdims=True))
            return tuple(next_maxes), tuple(new_ms), tuple(new_accs)

        ms = tuple(jnp.full((1, tq), NEG_INF, F32) for _ in range(2))
        accs = tuple(jnp.zeros((V_ROWS, tq), F32) for _ in range(2))
        bufs = (s_a, s_b)
        maxes = first_scores(s_a)
        n_loop = (n_chunks - 1) // CHUNK_UNROLL

        def body(i, carry):
            maxes, ms, accs = carry
            for r in range(CHUNK_UNROLL):
                maxes, ms, accs = step(CHUNK_UNROLL * i + r, bufs[r % 2], bufs[(r + 1) % 2], maxes, ms, accs, True)
            return maxes, ms, accs

        maxes, ms, accs = lax.fori_loop(0, n_loop, body, (maxes, ms, accs))
        for c in range(n_loop * CHUNK_UNROLL, n_chunks):
            maxes, ms, accs = step(c, bufs[c % 2], bufs[(c + 1) % 2], maxes, ms, accs, c + 1 < n_chunks)
        finish(accs)


def _mla_attn(geom, qt, k, vt, need_ctx):
    b, tpb, tm = geom.b, geom.tpb, geom.tm
    cs = 3 if tpb % 3 == 0 else (2 if tpb % 2 == 0 else 1)
    t0 = 0 if need_ctx else 1
    nt = tpb - t0

    def q_map(bi, hp, t):
        return (bi * tpb + t + t0, hp, 0)

    return pl.pallas_call(
        functools.partial(_mla_attn_kernel, cs=cs, n_chunks=tpb // cs, t0=t0),
        grid=(b, MLA_HEADS // 2, nt),
        in_specs=[pl.BlockSpec((1, 2 * LANES, tm), q_map),
                  pl.BlockSpec((tpb * tm, 2 * LANES), lambda bi, hp, t: (bi, hp)),
                  pl.BlockSpec((tpb, 2 * LANES, tm), lambda bi, hp, t: (bi, hp, 0))],
        out_specs=pl.BlockSpec((1, 2 * HEAD_V, tm), q_map),
        out_shape=jax.ShapeDtypeStruct((geom.tiles, MLA_HEADS * HEAD_V, tm), BF16),
        scratch_shapes=[pltpu.VMEM((2, cs * tm, tm), F32), pltpu.VMEM((2, cs * tm, tm), F32)],
        compiler_params=_cparams(3),
        name="mla_attn",
    )(qt, k, vt)


def _prep_swa_weights(w_qkv):
    d = w_qkv.shape[0]
    dq = SWA_Q_HEADS * SWA_HEAD_DIM
    dkv = SWA_KV_HEADS * SWA_HEAD_DIM

    def rope_pair(w, heads):
        w3 = w.reshape(d, heads, SWA_HEAD_DIM)
        e, o = w3[:, :, 0::2], w3[:, :, 1::2]
        z = jnp.zeros((d, heads, LANES - SWA_HEAD_DIM), F32)
        return (jnp.concatenate([e, o, z], axis=-1).reshape(d, heads * LANES),
                jnp.concatenate([o, e, z], axis=-1).reshape(d, heads * LANES))

    wqa, wqb = rope_pair(w_qkv[:, :dq], SWA_Q_HEADS)
    wka, wkb = rope_pair(w_qkv[:, dq:dq + dkv], SWA_KV_HEADS)
    v3 = w_qkv[:, dq + dkv:].reshape(d, SWA_KV_HEADS, SWA_HEAD_DIM)
    wv = jnp.concatenate([v3, jnp.zeros((d, SWA_KV_HEADS, LANES - SWA_HEAD_DIM), F32)], axis=-1)
    wv = wv.reshape(d, SWA_KV_HEADS * LANES)
    return jnp.concatenate([wqa, wqb, wka, wkb, wv], axis=1).astype(BF16)


def _swa_pre_kernel(x_ref, mod_ref, g_ref, w_ref, cq_ref, sq_ref, ck_ref, sk_ref, ones_ref,
                    q_out, k_out, v_out):
    mod = mod_ref[...]
    h = _norm_mod(x_ref[...], g_ref[...], mod[:, :D_MODEL], mod[:, D_MODEL:2 * D_MODEL]).astype(BF16)
    qw = SWA_Q_HEADS * LANES
    kw = SWA_KV_HEADS * LANES
    qab = jnp.dot(h, w_ref[:, :2 * qw], preferred_element_type=F32)
    cq, sq = cq_ref[...], sq_ref[...]
    for hh in range(SWA_Q_HEADS):
        lo = hh * LANES
        q_out[:, lo:lo + LANES] = (qab[:, lo:lo + LANES] * cq
                                   + qab[:, qw + lo:qw + lo + LANES] * sq).astype(BF16)
    kv = jnp.dot(h, w_ref[:, 2 * qw:], preferred_element_type=F32)
    ck, sk = ck_ref[...], sk_ref[...]
    for hh in range(SWA_KV_HEADS):
        lo = hh * LANES
        k_out[:, lo:lo + LANES] = (kv[:, lo:lo + LANES] * ck
                                   + kv[:, kw + lo:kw + lo + LANES] * sk).astype(BF16)
    v_out[...] = (kv[:, 2 * kw:] + ones_ref[...]).astype(BF16)


def _swa_pre(geom, layer, xall, mods, g, w_qkv):
    w = _prep_swa_weights(w_qkv)
    scale = SWA_HEAD_DIM ** -0.5
    cq, sq = _rope_tables(geom, SWA_HEAD_DIM, 0, scale)
    ck, sk = _rope_tables(geom, SWA_HEAD_DIM, 0, 1.0)
    qw, kw = SWA_Q_HEADS * LANES, SWA_KV_HEADS * LANES
    return pl.pallas_call(
        _swa_pre_kernel,
        grid=(geom.tiles,),
        in_specs=[
            _row_spec(geom, D_MODEL), _mod_spec(geom, layer, lambda t: t), _const_spec((1, D_MODEL)),
            _const_spec(w.shape),
            _pos_spec(geom), _pos_spec(geom), _pos_spec(geom), _pos_spec(geom),
            _const_spec((1, kw)),
        ],
        out_specs=[_row_spec(geom, qw), _row_spec(geom, kw), _row_spec(geom, kw)],
        out_shape=[jax.ShapeDtypeStruct((geom.rows, qw), BF16),
                   jax.ShapeDtypeStruct((geom.rows, kw), BF16),
                   jax.ShapeDtypeStruct((geom.rows, kw), BF16)],
        compiler_params=_cparams(1),
        name="swa_pre",
    )(xall, mods, g.reshape(1, D_MODEL), w, cq, sq, ck, sk,
      jnp.asarray(_ones_slot_vector(SWA_KV_HEADS).reshape(1, kw)))


def _prep_na_weights(w_qkv):
    d = w_qkv.shape[0]
    dm = NA_HEADS * NA_HEAD_DIM
    z = jnp.zeros((d, NA_HEADS, LANES - NA_HEAD_DIM), F32)

    def slots(w):
        return jnp.concatenate([w.reshape(d, NA_HEADS, NA_HEAD_DIM), z], axis=-1).reshape(d, NA_HEADS * LANES)

    scale = NA_HEAD_DIM ** -0.5
    return jnp.concatenate([slots(w_qkv[:, :dm]) * scale, slots(w_qkv[:, dm:2 * dm]),
                            slots(w_qkv[:, 2 * dm:])], axis=1).astype(BF16)


def _na_pre_kernel(x_ref, mod_ref, g_ref, w_ref, ones_ref, q_out, k_out, v_out):
    mod = mod_ref[...]
    h = _norm_mod(x_ref[...], g_ref[...], mod[:, :D_MODEL], mod[:, D_MODEL:2 * D_MODEL]).astype(BF16)
    hw = NA_HEADS * LANES
    q_out[...] = jnp.dot(h, w_ref[:, :hw], preferred_element_type=F32).astype(BF16)
    k_out[...] = jnp.dot(h, w_ref[:, hw:2 * hw], preferred_element_type=F32).astype(BF16)
    v_out[...] = (jnp.dot(h, w_ref[:, 2 * hw:], preferred_element_type=F32) + ones_ref[...]).astype(BF16)


def _na_pre(geom, layer, xall, mods, g, w_qkv):
    w = _prep_na_weights(w_qkv)
    hw = NA_HEADS * LANES
    out = jax.ShapeDtypeStruct((geom.rows, hw), BF16)
    return pl.pallas_call(
        _na_pre_kernel,
        grid=(geom.tiles,),
        in_specs=[_row_spec(geom, D_MODEL), _mod_spec(geom, layer, lambda t: t), _const_spec((1, D_MODEL)),
                  _const_spec(w.shape), _const_spec((1, hw))],
        out_specs=[_row_spec(geom, hw)] * 3,
        out_shape=[out, out, out],
        compiler_params=_cparams(1),
        name="na_pre",
    )(xall, mods, g.reshape(1, D_MODEL), w, jnp.asarray(_ones_slot_vector(NA_HEADS).reshape(1, hw)))


def _band_kernel(*refs, tq, hbq, grp, has_sink):
    if has_sink:
        q_ref, kp, kc, kn, kx, vp, vc, vn, vx, bias_ref, sink_ref, o_ref = refs
    else:
        q_ref, kp, kc, kn, kx, vp, vc, vn, vx, bias_ref, o_ref = refs
    hbk = hbq // grp
    shared_bias = bias_ref.shape[1] == 1
    dn = (((1,), (1,)), ((), ()))
    head0 = pl.program_id(1) * hbq
    outs = []
    for j in range(hbk):
        ls = slice(j * LANES, (j + 1) * LANES)
        kcat = jnp.concatenate([kp[:, ls], kc[:, ls], kn[:, ls], kx[:, ls]], axis=0)
        vcat = jnp.concatenate([vp[:, ls], vc[:, ls], vn[:, ls], vx[:, ls]], axis=0)
        width = kcat.shape[0]
        if grp > 1:
            qs = jnp.concatenate([q_ref[:, (j * grp + i) * LANES:(j * grp + i + 1) * LANES]
                                  for i in range(grp)], axis=0)
        else:
            qs = q_ref[:, ls]
        s = lax.dot_general(qs, kcat, dn, preferred_element_type=F32)
        if shared_bias:
            s = (s.reshape(grp, tq, width) + bias_ref[0, 0][None]).reshape(grp * tq, width)
        else:
            s = s + bias_ref[0, j]
        m = jnp.max(s, axis=-1, keepdims=True)
        if has_sink:
            row_head = lax.broadcasted_iota(jnp.int32, (grp * tq, 1), 0) // tq
            sink_col = jnp.zeros((grp * tq, 1), F32)
            for i in range(grp):
                sink_col = jnp.where(row_head == i, sink_ref[head0 + j * grp + i], sink_col)
            m = jnp.maximum(m, sink_col)
        p = jnp.exp(s - m)
        acc = jnp.dot(p.astype(BF16), vcat, preferred_element_type=F32)
        denom = acc[:, HEAD_V:HEAD_V + 1]
        if has_sink:
            denom = denom + jnp.exp(sink_col - m)
        o = acc * (1.0 / denom)
        for i in range(grp):
            outs.append(o[i * tq:(i + 1) * tq])
    for pi in range(hbq // 2):
        o_ref[:, pi * LANES:(pi + 1) * LANES] = _pair_heads(outs[2 * pi], outs[2 * pi + 1]).astype(BF16)


def _band_attn(geom, q, k, v, bias, sink, *, tq, hbq, grp, need_ctx, name):
    b, n, l = geom.b, geom.n, geom.l
    n_q_heads = q.shape[1] // LANES
    hbk = hbq // grp
    n_groups = n_q_heads // hbq
    tiles_x = n // tq
    ctx_tiles = l // tq
    batch_tiles = tiles_x + ctx_tiles
    nt = tiles_x + (ctx_tiles if need_ctx else 0)

    def q_map(bi, g, t):
        return (bi * batch_tiles + jnp.where(t < tiles_x, ctx_tiles + t, t - tiles_x), g)

    def band_map(delta):
        def index(bi, g, t):
            tt = jnp.clip(jnp.minimum(t, tiles_x - 1) + delta, 0, tiles_x - 1)
            return (bi * batch_tiles + ctx_tiles + tt, g)
        return index

    def bias_map(bi, g, t):
        variant = jnp.where(t >= tiles_x, 3, jnp.where(t == 0, 0, jnp.where(t == tiles_x - 1, 2, 1)))
        return (variant, g if bias.shape[1] > 1 else 0, 0, 0)

    band_specs = [pl.BlockSpec((tq, hbk * LANES), band_map(dl)) for dl in (-1, 0, 1)]
    ctx_spec = pl.BlockSpec((l, hbk * LANES), lambda bi, g, t: (bi * geom.tpb, g))
    hb_bias = hbq if bias.shape[1] > 1 else 1
    in_specs = ([pl.BlockSpec((tq, hbq * LANES), q_map)] + band_specs + [ctx_spec] + band_specs + [ctx_spec]
                + [pl.BlockSpec((1, hb_bias, tq, 3 * tq + l), bias_map)])
    args = [q, k, k, k, k, v, v, v, v, bias]
    if sink is not None:
        in_specs.append(pl.BlockSpec(memory_space=pltpu.SMEM))
        args.append(sink)
    return pl.pallas_call(
        functools.partial(_band_kernel, tq=tq, hbq=hbq, grp=grp, has_sink=sink is not None),
        grid=(b, n_groups, nt),
        in_specs=in_specs,
        out_specs=pl.BlockSpec((tq, hbq * HEAD_V), q_map),
        out_shape=jax.ShapeDtypeStruct((geom.rows, n_q_heads * HEAD_V), BF16),
        compiler_params=_cparams(3),
        name=name,
    )(*args)


def _swa_bias(l):
    tq = SWA_BLOCK
    i = np.arange(tq)[:, None]
    j = np.arange(tq)[None, :]
    prev_ok = (SWA_BLOCK + i - j) <= SWA_WINDOW
    next_ok = (SWA_BLOCK + j - i) <= SWA_WINDOW
    cur_ok = np.abs(i - j) <= SWA_WINDOW
    none = np.zeros((tq, tq), bool)
    ctx_ok = np.ones((tq, l), bool)
    variants = [
        np.concatenate([none, cur_ok, next_ok, ctx_ok], axis=1),
        np.concatenate([prev_ok, cur_ok, next_ok, ctx_ok], axis=1),
        np.concatenate([prev_ok, cur_ok, none, ctx_ok], axis=1),
        np.concatenate([none, none, none, ctx_ok], axis=1),
    ]
    ok = np.stack(variants)[:, None]
    return jnp.asarray(np.where(ok, 0.0, NEG_INF).astype(np.float32))


NA_TILE_ROWS = 4


def _na_bias(rel_bias, l):
    col = np.arange(GRID_W)
    col_start = np.clip(col - NA_COLS // 2, 0, GRID_W - NA_COLS)
    col_valid = (col[None, :] >= col_start[:, None]) & (col[None, :] < col_start[:, None] + NA_COLS)
    dc_idx = np.clip(col[None, :] - col[:, None] + NA_COLS - 1, 0, 2 * NA_COLS - 2)
    bias_cols = rel_bias.astype(F32)[:, :, dc_idx]
    bias_cols = jnp.where(col_valid, bias_cols, NEG_INF)
    r = NA_TILE_ROWS
    a = np.arange(r)[:, None]
    c = np.arange(3 * r)[None, :]
    dr_idx = np.clip(c - r - a + NA_ROWS - 1, 0, 2 * NA_ROWS - 2)
    tiles = bias_cols[:, dr_idx]
    tiles = tiles.transpose(0, 1, 3, 2, 4)
    half = NA_ROWS // 2
    interior = (c - r >= a - half) & (c - r < a - half + NA_ROWS)
    first = (c >= r) & (c < r + NA_ROWS) & (a >= 0)
    last = (c >= 2 * r - NA_ROWS) & (c < 2 * r) & (a >= 0)
    none = np.zeros_like(interior)
    h = rel_bias.shape[0]
    out = []
    for ok in (first, interior, last, none):
        ok5 = ok[None, :, None, :, None]
        t = jnp.where(ok5, tiles, NEG_INF).reshape(h, r * GRID_W, 3 * r * GRID_W)
        out.append(jnp.concatenate([t, jnp.zeros((h, r * GRID_W, l), F32)], axis=-1))
    return jnp.stack(out)


def _post_kernel(*refs, final, o_transposed):
    if final:
        x_ref, o_ref, mod_ref, wo_ref, g_ref, w1_ref, w2_ref, gout_ref, out_ref = refs
    else:
        x_ref, o_ref, mod_ref, wo_ref, g_ref, w1_ref, w2_ref, out_ref = refs
    d = D_MODEL
    mod = mod_ref[...]
    g1, sh2, sc2, g2 = mod[:, 2 * d:3 * d], mod[:, 3 * d:4 * d], mod[:, 4 * d:5 * d], mod[:, 5 * d:6 * d]
    if o_transposed:
        y = lax.dot_general(o_ref[0], wo_ref[...], (((0,), (0,)), ((), ())), preferred_element_type=F32)
    else:
        y = jnp.dot(o_ref[...], wo_ref[...], preferred_element_type=F32)
    x1 = x_ref[...] + g1 * y
    h2 = _norm_mod(x1, g_ref[...], sh2, sc2).astype(BF16)
    acc = jnp.zeros(x1.shape, F32)
    for c in range(D_FF // d):
        u = jnp.maximum(jnp.dot(h2, w1_ref[:, c * d:(c + 1) * d], preferred_element_type=F32), 0.0)
        acc = acc + jnp.dot((u * u).astype(BF16), w2_ref[c * d:(c + 1) * d, :], preferred_element_type=F32)
    x2 = x1 + g2 * acc
    if final:
        x2 = _rms(x2) * gout_ref[...]
    out_ref[...] = x2


def _post(geom, layer, xall, o, o_transposed, mods, w_o, g, w1, w2, g_out):
    final = g_out is not None
    tm = geom.tm
    if final:
        tiles = geom.b * geom.xtpb
        tile_of = lambda t: (t // geom.xtpb) * geom.tpb + 1 + t % geom.xtpb
    else:
        tiles = geom.tiles
        tile_of = lambda t: t
    x_spec = pl.BlockSpec((tm, D_MODEL), lambda t: (tile_of(t), 0))
    if o_transposed:
        o_spec = pl.BlockSpec((1, D_MODEL, tm), lambda t: (tile_of(t), 0, 0))
    else:
        o_spec = x_spec
    in_specs = [x_spec, o_spec, _mod_spec(geom, layer, tile_of),
                _const_spec(w_o.shape), _const_spec((1, D_MODEL)), _const_spec(w1.shape), _const_spec(w2.shape)]
    args = [xall, o, mods, w_o.astype(BF16), g.reshape(1, D_MODEL), w1.astype(BF16), w2.astype(BF16)]
    if final:
        in_specs.append(_const_spec((1, D_MODEL)))
        args.append(g_out.reshape(1, D_MODEL))
    return pl.pallas_call(
        functools.partial(_post_kernel, final=final, o_transposed=o_transposed),
        grid=(tiles,),
        in_specs=in_specs,
        out_specs=pl.BlockSpec((tm, D_MODEL), lambda t: (t, 0)),
        out_shape=jax.ShapeDtypeStruct((tiles * tm, D_MODEL), F32),
        compiler_params=_cparams(1),
        name="post_mlp",
    )(*args)


def kernel(x, c, ctx, c_ctx, ada_w, ada_b, norm_mix_g, norm_mlp_g, norm_out_g, mlp_w1, mlp_w2,
           mla_w_in, mla_qa_g, mla_w_qb, mla_kva_g, mla_w_kvb, mla_w_o,
           swa_w_qkv, swa_sink, swa_w_o, na_w_qkv, na_rel_bias, na_w_o):
    b, n, d = x.shape
    l = ctx.shape[1]
    assert d == D_MODEL and n % GRID_W == 0 and l % SWA_BLOCK == 0 and l == NA_TILE_ROWS * GRID_W
    geom = _Geom(b, n, l)

    mod_rows = -(-(b + 1) // 8) * 8
    c_rows = jnp.concatenate([c, c_ctx[None, :], jnp.zeros((mod_rows - b - 1, d), F32)], axis=0)
    mods = _ada_mods(c_rows, ada_w, ada_b).reshape(DEPTH, mod_rows, 1, N_MOD * d)

    xall = jnp.concatenate([ctx, x], axis=1).reshape(geom.rows, d)
    for i in range(DEPTH):
        need_ctx = i < DEPTH - 1
        kind, j = i % 3, i // 3
        if kind == 0:
            qt, k, vt = _mla_pre(geom, i, xall, mods, norm_mix_g[i], mla_w_in[j], mla_qa_g[j], mla_w_qb[j],
                                 mla_kva_g[j], mla_w_kvb[j])
            o = _mla_attn(geom, qt, k, vt, need_ctx)
            w_o = mla_w_o[j]
        elif kind == 1:
            q, k, v = _swa_pre(geom, i, xall, mods, norm_mix_g[i], swa_w_qkv[j])
            o = _band_attn(geom, q, k, v, _swa_bias(l), swa_sink[j], tq=SWA_BLOCK, hbq=SWA_Q_HEADS,
                           grp=SWA_Q_HEADS // SWA_KV_HEADS, need_ctx=need_ctx, name="swa_attn")
            w_o = swa_w_o[j]
        else:
            q, k, v = _na_pre(geom, i, xall, mods, norm_mix_g[i], na_w_qkv[j])
            o = _band_attn(geom, q, k, v, _na_bias(na_rel_bias[j], l), None, tq=NA_TILE_ROWS * GRID_W,
                           hbq=4, grp=1, need_ctx=need_ctx, name="na_attn")
            w_o = na_w_o[j]
        g_out = norm_out_g if i == DEPTH - 1 else None
        xall = _post(geom, i, xall, o, kind == 0, mods, w_o, norm_mlp_g[i], mlp_w1[i], mlp_w2[i], g_out)
    return xall.reshape(b, n, d)
```

```python
import functools
import math

import numpy as np
import jax
import jax.numpy as jnp
from jax import lax
from jax.experimental import pallas as pl
from jax.experimental.pallas import tpu as pltpu

F32 = jnp.float32
BF16 = jnp.bfloat16

D_MODEL = 1024
D_FF = 4 * D_MODEL
DEPTH = 4
N_MOD = 6
GRID_W = 64
EPS = 1e-6
ROPE_BASE = 10000.0
NEG_INF = -1e30

MLA_HEADS = 16
MLA_Q_RANK = 256
MLA_KV_RANK = 128
MLA_NOPE = 64
MLA_ROPE = 32
MLA_V = 64

SWA_Q_HEADS = 16
SWA_KV_HEADS = 4
SWA_HEAD_DIM = 64
SWA_WINDOW = 128
SWA_BLOCK = 128

NA_HEADS = 16
NA_HEAD_DIM = 64
NA_ROWS = 8
NA_COLS = 16

LANES = 128
HEAD_V = 64
CHUNK_UNROLL = 4
V_ROWS = 80
VMEM_LIMIT = 56 * 1024 * 1024
LOG2E = math.log2(math.e)


def _cparams(n_axes):
    return pltpu.CompilerParams(dimension_semantics=("arbitrary",) * n_axes,
                                vmem_limit_bytes=VMEM_LIMIT)


def _const_spec(shape):
    nd = len(shape)
    return pl.BlockSpec(shape, lambda *_: (0,) * nd, pipeline_mode=pl.Buffered(1))


def _rms(x):
    return x * lax.rsqrt(jnp.mean(x * x, axis=-1, keepdims=True) + EPS)


def _norm_mod(x, g, shift, scale):
    return (_rms(x) * g) * (1.0 + scale) + shift


def _ada_kernel(c_ref, w_ref, b_ref, o_ref):
    c = c_ref[...]
    act = c / (1.0 + jnp.exp(-c))
    o_ref[0] = jnp.dot(act.astype(BF16), w_ref[0], preferred_element_type=F32) + b_ref[0]


def _ada_mods(c_rows, ada_w, ada_b):
    depth, d, n = ada_w.shape
    tn = 1536
    return pl.pallas_call(
        _ada_kernel,
        grid=(depth, n // tn),
        in_specs=[
            pl.BlockSpec((c_rows.shape[0], d), lambda i, j: (0, 0)),
            pl.BlockSpec((1, d, tn), lambda i, j: (i, 0, j)),
            pl.BlockSpec((1, 1, tn), lambda i, j: (i, 0, j)),
        ],
        out_specs=pl.BlockSpec((1, c_rows.shape[0], tn), lambda i, j: (i, 0, j)),
        out_shape=jax.ShapeDtypeStruct((depth, c_rows.shape[0], n), F32),
        compiler_params=_cparams(2),
        name="ada_mods",
    )(c_rows, ada_w.astype(BF16), ada_b.reshape(depth, 1, n))


class _Geom:
    def __init__(self, b, n, l):
        assert n % l == 0
        self.b, self.n, self.l = b, n, l
        self.tm = l
        self.tpb = (n + l) // l
        self.xtpb = n // l
        self.tiles = b * self.tpb
        self.rows = self.tiles * self.tm

    def mod_row(self, t):
        return jnp.where(t % self.tpb == 0, self.b, t // self.tpb)

    def pos_block(self, t):
        return t % self.tpb


def _mod_spec(geom, layer, tile_of):
    return pl.BlockSpec((None, None, 1, N_MOD * D_MODEL),
                        lambda t: (layer, geom.mod_row(tile_of(t)), 0, 0))


def _row_spec(geom, width):
    return pl.BlockSpec((geom.tm, width), lambda t: (t, 0))


def _pos_spec(geom):
    return pl.BlockSpec((geom.tm, LANES), lambda t: (geom.pos_block(t), 0))


def _pos_spec_t(geom):
    return pl.BlockSpec((LANES, geom.tm), lambda t: (0, geom.pos_block(t)))


def _rope_cos_sin(n, rot_dim):
    t = jnp.arange(n, dtype=jnp.int32)
    row = (t // GRID_W).astype(F32)
    col = (t % GRID_W).astype(F32)
    n_freq = rot_dim // 4
    inv_freq = ROPE_BASE ** (-jnp.arange(n_freq, dtype=F32) / n_freq)
    ang = jnp.concatenate([row[:, None] * inv_freq, col[:, None] * inv_freq], axis=-1)
    return jnp.cos(ang), jnp.sin(ang)


def _rope_tables(geom, rot_dim, lead, scale):
    c, s = _rope_cos_sin(geom.n, rot_dim)
    half = rot_dim // 2
    pad = LANES - lead - rot_dim
    n, l = geom.n, geom.l
    cos_t = jnp.concatenate([jnp.ones((n, lead), F32), c, c, jnp.zeros((n, pad), F32)], axis=1)
    sin_t = jnp.concatenate([jnp.zeros((n, lead), F32), -s, s, jnp.zeros((n, pad), F32)], axis=1)
    cos_c = jnp.concatenate([jnp.ones((l, lead + 2 * half), F32), jnp.zeros((l, pad), F32)], axis=1)
    sin_c = jnp.zeros((l, LANES), F32)
    return (jnp.concatenate([cos_c, cos_t], axis=0) * scale,
            jnp.concatenate([sin_c, sin_t], axis=0) * scale)


def _ones_slot_vector(n_slots):
    row = np.zeros((n_slots, LANES), np.float32)
    row[:, HEAD_V] = 1.0
    return row.reshape(n_slots * LANES)


def _prep_mla_weights(w_in, w_qb, w_kvb):
    d = w_in.shape[0]
    pe0 = MLA_Q_RANK + MLA_KV_RANK
    ev = pe0 + 2 * np.arange(MLA_ROPE // 2)
    od = ev + 1
    z = jnp.zeros((d, LANES - MLA_ROPE), F32)
    win = jnp.concatenate([w_in[:, :pe0], w_in[:, ev], w_in[:, od], z, w_in[:, od], w_in[:, ev], z], axis=1)

    hd = MLA_NOPE + MLA_ROPE
    wq3 = w_qb.reshape(MLA_Q_RANK, MLA_HEADS, hd)
    nope, pe_e, pe_o = wq3[:, :, :MLA_NOPE], wq3[:, :, MLA_NOPE::2], wq3[:, :, MLA_NOPE + 1::2]
    zpad = jnp.zeros((MLA_Q_RANK, MLA_HEADS, LANES - hd), F32)
    zlead = jnp.zeros((MLA_Q_RANK, MLA_HEADS, MLA_NOPE), F32)
    wqa = jnp.concatenate([nope, pe_e, pe_o, zpad], axis=-1).reshape(MLA_Q_RANK, MLA_HEADS * LANES)
    wqb = jnp.concatenate([zlead, pe_o, pe_e, zpad], axis=-1).reshape(MLA_Q_RANK, MLA_HEADS * LANES)
    wq_t = jnp.concatenate([wqa, wqb], axis=1).T

    wkv3 = w_kvb.reshape(MLA_KV_RANK, MLA_HEADS, MLA_NOPE + MLA_V)
    zk = jnp.zeros((MLA_KV_RANK, MLA_HEADS, LANES - MLA_NOPE), F32)
    wk_top = jnp.concatenate([wkv3[:, :, :MLA_NOPE], zk], axis=-1).reshape(MLA_KV_RANK, MLA_HEADS * LANES)
    eye = np.zeros((LANES, MLA_HEADS, LANES), np.float32)
    for j in range(MLA_ROPE):
        eye[j, :, MLA_NOPE + j] = 1.0
    wk = jnp.concatenate([wk_top, jnp.asarray(eye.reshape(LANES, MLA_HEADS * LANES))], axis=0)
    zv = jnp.zeros((MLA_KV_RANK, MLA_HEADS, LANES - MLA_V), F32)
    wv_t = jnp.concatenate([wkv3[:, :, MLA_NOPE:], zv], axis=-1).reshape(MLA_KV_RANK, MLA_HEADS * LANES).T
    return win.astype(BF16), wq_t.astype(BF16), wk.astype(BF16), wv_t.astype(BF16)


def _mla_pre_kernel(x_ref, mod_ref, g_ref, win_ref, qag_ref, kvag_ref, wqt_ref, wk_ref, wvt_ref,
                    cqt_ref, sqt_ref, ck_ref, sk_ref, ones_ref, qt_out, k_out, vt_out):
    nt = (((1,), (1,)), ((), ()))
    mod = mod_ref[...]
    h = _norm_mod(x_ref[...], g_ref[...], mod[:, :D_MODEL], mod[:, D_MODEL:2 * D_MODEL]).astype(BF16)
    proj = jnp.dot(h, win_ref[...], preferred_element_type=F32)
    kv0 = MLA_Q_RANK
    pe0 = kv0 + MLA_KV_RANK
    qn = (_rms(proj[:, :kv0]) * qag_ref[...]).astype(BF16)
    kvn = _rms(proj[:, kv0:pe0]) * kvag_ref[...]
    qab_t = lax.dot_general(wqt_ref[...], qn, nt, preferred_element_type=F32)
    cqt, sqt = cqt_ref[...], sqt_ref[...]
    half = MLA_HEADS * LANES
    for hh in range(MLA_HEADS):
        lo = hh * LANES
        qt_out[0, lo:lo + LANES, :] = (qab_t[lo:lo + LANES] * cqt
                                       + qab_t[half + lo:half + lo + LANES] * sqt).astype(BF16)
    kpe = proj[:, pe0:pe0 + LANES] * ck_ref[...] + proj[:, pe0 + LANES:pe0 + 2 * LANES] * sk_ref[...]
    lat = jnp.concatenate([kvn, kpe], axis=1).astype(BF16)
    k_out[...] = jnp.dot(lat, wk_ref[...], preferred_element_type=F32).astype(BF16)
    vt = lax.dot_general(wvt_ref[...], kvn.astype(BF16), nt, preferred_element_type=F32)
    vt_out[0] = (vt + ones_ref[...]).astype(BF16)


def _mla_pre(geom, layer, xall, mods, g, w_in, qa_g, w_qb, kva_g, w_kvb):
    win, wq_t, wk, wv_t = _prep_mla_weights(w_in, w_qb, w_kvb)
    scale = (MLA_NOPE + MLA_ROPE) ** -0.5 * LOG2E
    cq, sq = _rope_tables(geom, MLA_ROPE, MLA_NOPE, scale)
    ck, sk = _rope_tables(geom, MLA_ROPE, 0, 1.0)
    width = MLA_HEADS * LANES
    tm = geom.tm
    t_spec = pl.BlockSpec((1, width, tm), lambda t: (t, 0, 0))
    t_shape = jax.ShapeDtypeStruct((geom.tiles, width, tm), BF16)
    return pl.pallas_call(
        _mla_pre_kernel,
        grid=(geom.tiles,),
        in_specs=[
            _row_spec(geom, D_MODEL), _mod_spec(geom, layer, lambda t: t), _const_spec((1, D_MODEL)),
            _const_spec(win.shape), _const_spec((1, MLA_Q_RANK)), _const_spec((1, MLA_KV_RANK)),
            _const_spec(wq_t.shape), _const_spec(wk.shape), _const_spec(wv_t.shape),
            _pos_spec_t(geom), _pos_spec_t(geom), _pos_spec(geom), _pos_spec(geom),
            _const_spec((width, 1)),
        ],
        out_specs=[t_spec, _row_spec(geom, width), t_spec],
        out_shape=[t_shape, jax.ShapeDtypeStruct((geom.rows, width), BF16), t_shape],
        compiler_params=_cparams(1),
        name="mla_pre",
    )(xall, mods, g.reshape(1, D_MODEL), win, qa_g.reshape(1, -1), kva_g.reshape(1, -1), wq_t, wk, wv_t,
      cq.T, sq.T, ck, sk, jnp.asarray(_ones_slot_vector(MLA_HEADS).reshape(width, 1)))


def _mla_attn_kernel(qt_ref, k_ref, vt_ref, ot_ref, s_a, s_b, *, cs, n_chunks, t0):
    tm = vt_ref.shape[2]
    tk = cs * tm
    t = pl.program_id(2) + t0

    def finish(accs):
        outs = [acc[:HEAD_V] * (1.0 / acc[HEAD_V:HEAD_V + 1]) for acc in accs]
        ot_ref[0] = jnp.concatenate(outs, axis=0).astype(BF16)

    def _context_tile():
        accs = []
        for j in range(2):
            s = jnp.dot(k_ref[0:tm, j * LANES:(j + 1) * LANES], qt_ref[0, j * LANES:(j + 1) * LANES, :],
                        preferred_element_type=F32)
            p = jnp.exp2(s - jnp.max(s, axis=0, keepdims=True)).astype(BF16)
            accs.append(jnp.dot(vt_ref[0, j * LANES:j * LANES + V_ROWS, :], p, preferred_element_type=F32))
        finish(accs)

    if t0 == 0:
        pl.when(t == 0)(_context_tile)

    @pl.when(t > 0)
    def _latent_tile():
        qts = [qt_ref[0, j * LANES:(j + 1) * LANES, :] for j in range(2)]

        tq = qt_ref.shape[2]

        def qk(c, j, u, s_ref):
            r0 = pl.multiple_of((c * cs + u) * tm, tm)
            s = jnp.dot(k_ref[pl.ds(r0, tm), j * LANES:(j + 1) * LANES], qts[j], preferred_element_type=F32)
            s_ref[j, u * tm:(u + 1) * tm, :] = s
            return jnp.max(s.reshape(tm // 8, 8, tq), axis=0)

        def pv(c, j, u, s_ref, m_new):
            p = jnp.exp2(s_ref[j, u * tm:(u + 1) * tm, :] - m_new).astype(BF16)
            return jnp.dot(vt_ref[c * cs + u, j * LANES:j * LANES + V_ROWS, :], p, preferred_element_type=F32)

        def first_scores(s_ref):
            maxes = []
            for j in range(2):
                part = qk(0, j, 0, s_ref)
                for u in range(1, cs):
                    part = jnp.maximum(part, qk(0, j, u, s_ref))
                maxes.append(jnp.max(part, axis=0, keepdims=True))
            return tuple(maxes)

        def step(c, s_cur, s_next, maxes, ms, accs, with_next):
            new_ms, new_accs, next_maxes = [], [], []
            for j in range(2):
                m_new = jnp.maximum(ms[j], maxes[j])
                alpha = jnp.exp2(ms[j] - m_new)
                part = qk(c + 1, j, 0, s_next) if with_next else None
                acc_new = None
                for u in range(cs):
                    if with_next and u + 1 < cs:
                        part = jnp.maximum(part, qk(c + 1, j, u + 1, s_next))
                    d = pv(c, j, u, s_cur, m_new)
                    acc_new = d if acc_new is None else acc_new + d
                new_accs.append(alpha * accs[j] + acc_new)
                new_ms.append(m_new)
                if with_next:
                    next_maxes.append(jnp.max(part, axis=0, keepdims=True))
            return tuple(next_maxes), tuple(new_ms), tuple(new_accs)

        ms = tuple(jnp.full((1, tq), NEG_INF, F32) for _ in range(2))
        accs = tuple(jnp.zeros((V_ROWS, tq), F32) for _ in range(2))
        bufs = (s_a, s_b)
        maxes = first_scores(s_a)
        n_loop = (n_chunks - 1) // CHUNK_UNROLL

        def body(i, carry):
            maxes, ms, accs = carry
            for r in range(CHUNK_UNROLL):
                maxes, ms, accs = step(CHUNK_UNROLL * i + r, bufs[r % 2], bufs[(r + 1) % 2], maxes, ms, accs, True)
            return maxes, ms, accs

        maxes, ms, accs = lax.fori_loop(0, n_loop, body, (maxes, ms, accs))
        for c in range(n_loop * CHUNK_UNROLL, n_chunks):
            maxes, ms, accs = step(c, bufs[c % 2], bufs[(c + 1) % 2], maxes, ms, accs, c + 1 < n_chunks)
        finish(accs)


def _mla_attn(geom, qt, k, vt, need_ctx):
    b, tpb, tm = geom.b, geom.tpb, geom.tm
    cs = 3 if tpb % 3 == 0 else (2 if tpb % 2 == 0 else 1)
    t0 = 0 if need_ctx else 1
    nt = tpb - t0

    def q_map(bi, hp, t):
        return (bi * tpb + t + t0, hp, 0)

    return pl.pallas_call(
        functools.partial(_mla_attn_kernel, cs=cs, n_chunks=tpb // cs, t0=t0),
        grid=(b, MLA_HEADS // 2, nt),
        in_specs=[pl.BlockSpec((1, 2 * LANES, tm), q_map),
                  pl.BlockSpec((tpb * tm, 2 * LANES), lambda bi, hp, t: (bi, hp)),
                  pl.BlockSpec((tpb, 2 * LANES, tm), lambda bi, hp, t: (bi, hp, 0))],
        out_specs=pl.BlockSpec((1, 2 * HEAD_V, tm), q_map),
        out_shape=jax.ShapeDtypeStruct((geom.tiles, MLA_HEADS * HEAD_V, tm), BF16),
        scratch_shapes=[pltpu.VMEM((2, cs * tm, tm), F32), pltpu.VMEM((2, cs * tm, tm), F32)],
        compiler_params=_cparams(3),
        name="mla_attn",
    )(qt, k, vt)


def _head_rope_tables(geom, scale):
    c, s = _rope_cos_sin(geom.n, SWA_HEAD_DIM)
    cos_t = jnp.concatenate([jnp.ones((geom.l, SWA_HEAD_DIM), F32), jnp.concatenate([c, c], axis=1)], axis=0)
    sin_t = jnp.concatenate([jnp.zeros((geom.l, SWA_HEAD_DIM), F32), jnp.concatenate([-s, s], axis=1)], axis=0)
    return cos_t * scale, sin_t * scale


def _prep_swa_weights(w_qkv):
    d = w_qkv.shape[0]
    dq = SWA_Q_HEADS * SWA_HEAD_DIM
    dkv = SWA_KV_HEADS * SWA_HEAD_DIM

    def rope_pair(w, heads):
        w3 = w.reshape(d, heads, SWA_HEAD_DIM)
        e, o = w3[:, :, 0::2], w3[:, :, 1::2]
        return (jnp.concatenate([e, o], axis=-1).reshape(d, heads * SWA_HEAD_DIM),
                jnp.concatenate([o, e], axis=-1).reshape(d, heads * SWA_HEAD_DIM))

    wqa, wqb = rope_pair(w_qkv[:, :dq], SWA_Q_HEADS)
    wka, wkb = rope_pair(w_qkv[:, dq:dq + dkv], SWA_KV_HEADS)
    w_t = jnp.concatenate([wqa, wqb, w_qkv[:, dq + dkv:]], axis=1).T
    return w_t.astype(BF16), jnp.concatenate([wka, wkb], axis=1).astype(BF16)


def _swa_pre_kernel(x_ref, mod_ref, g_ref, wt_ref, wk_ref, cqt_ref, sqt_ref, ck_ref, sk_ref,
                    qt_out, k_out, vt_out):
    nt = (((1,), (1,)), ((), ()))
    mod = mod_ref[...]
    h = _norm_mod(x_ref[...], g_ref[...], mod[:, :D_MODEL], mod[:, D_MODEL:2 * D_MODEL]).astype(BF16)
    tm = h.shape[0]
    dq = SWA_Q_HEADS * SWA_HEAD_DIM
    dkv = SWA_KV_HEADS * SWA_HEAD_DIM
    r = lax.dot_general(wt_ref[...], h, nt, preferred_element_type=F32)
    qa = r[:dq].reshape(SWA_Q_HEADS, SWA_HEAD_DIM, tm)
    qb = r[dq:2 * dq].reshape(SWA_Q_HEADS, SWA_HEAD_DIM, tm)
    qt_out[0] = (qa * cqt_ref[...][None] + qb * sqt_ref[...][None]).reshape(dq, tm).astype(BF16)
    vt_out[0] = r[2 * dq:].astype(BF16)
    kab = jnp.dot(h, wk_ref[...], preferred_element_type=F32)
    k_out[...] = (kab[:, :dkv] * ck_ref[...] + kab[:, dkv:] * sk_ref[...]).astype(BF16)


def _swa_pre(geom, layer, xall, mods, g, w_qkv):
    w_t, wk = _prep_swa_weights(w_qkv)
    cq, sq = _head_rope_tables(geom, SWA_HEAD_DIM ** -0.5)
    ck, sk = _head_rope_tables(geom, 1.0)
    ck, sk = jnp.tile(ck, (1, SWA_KV_HEADS)), jnp.tile(sk, (1, SWA_KV_HEADS))
    tm = geom.tm
    dq, dkv = SWA_Q_HEADS * SWA_HEAD_DIM, SWA_KV_HEADS * SWA_HEAD_DIM
    qt_pos = pl.BlockSpec((SWA_HEAD_DIM, tm), lambda t: (0, geom.pos_block(t)))
    k_pos = pl.BlockSpec((tm, dkv), lambda t: (geom.pos_block(t), 0))
    return pl.pallas_call(
        _swa_pre_kernel,
        grid=(geom.tiles,),
        in_specs=[
            _row_spec(geom, D_MODEL), _mod_spec(geom, layer, lambda t: t), _const_spec((1, D_MODEL)),
            _const_spec(w_t.shape), _const_spec(wk.shape), qt_pos, qt_pos, k_pos, k_pos,
        ],
        out_specs=[pl.BlockSpec((1, dq, tm), lambda t: (t, 0, 0)), _row_spec(geom, dkv),
                   pl.BlockSpec((1, dkv, tm), lambda t: (t, 0, 0))],
        out_shape=[jax.ShapeDtypeStruct((geom.tiles, dq, tm), BF16),
                   jax.ShapeDtypeStruct((geom.rows, dkv), BF16),
                   jax.ShapeDtypeStruct((geom.tiles, dkv, tm), BF16)],
        compiler_params=_cparams(1),
        name="swa_pre",
    )(xall, mods, g.reshape(1, D_MODEL), w_t, wk, cq.T, sq.T, ck, sk)


def _na_pre_kernel(x_ref, mod_ref, g_ref, wt_ref, wk_ref, qt_out, k_out, vt_out):
    nt = (((1,), (1,)), ((), ()))
    mod = mod_ref[...]
    h = _norm_mod(x_ref[...], g_ref[...], mod[:, :D_MODEL], mod[:, D_MODEL:2 * D_MODEL]).astype(BF16)
    dm = NA_HEADS * NA_HEAD_DIM
    r = lax.dot_general(wt_ref[...], h, nt, preferred_element_type=F32)
    qt_out[0] = r[:dm].astype(BF16)
    vt_out[0] = r[dm:].astype(BF16)
    k_out[...] = jnp.dot(h, wk_ref[...], preferred_element_type=F32).astype(BF16)


def _na_pre(geom, layer, xall, mods, g, w_qkv):
    dm = NA_HEADS * NA_HEAD_DIM
    scale = NA_HEAD_DIM ** -0.5
    w_t = jnp.concatenate([w_qkv[:, :dm] * scale, w_qkv[:, 2 * dm:]], axis=1).T.astype(BF16)
    wk = w_qkv[:, dm:2 * dm].astype(BF16)
    tm = geom.tm
    t_spec = pl.BlockSpec((1, dm, tm), lambda t: (t, 0, 0))
    t_shape = jax.ShapeDtypeStruct((geom.tiles, dm, tm), BF16)
    return pl.pallas_call(
        _na_pre_kernel,
        grid=(geom.tiles,),
        in_specs=[_row_spec(geom, D_MODEL), _mod_spec(geom, layer, lambda t: t), _const_spec((1, D_MODEL)),
                  _const_spec(w_t.shape), _const_spec(wk.shape)],
        out_specs=[t_spec, _row_spec(geom, dm), t_spec],
        out_shape=[t_shape, jax.ShapeDtypeStruct((geom.rows, dm), BF16), t_shape],
        compiler_params=_cparams(1),
        name="na_pre",
    )(xall, mods, g.reshape(1, D_MODEL), w_t, wk)


class _BandJob:
    def __init__(self, rhs, slot, vrow, bias, sink, write):
        self.rhs, self.slot, self.vrow, self.bias, self.sink, self.write = rhs, slot, vrow, bias, sink, write


def _slot_rhs(qt, upper):
    zeros = jnp.zeros_like(qt)
    return jnp.concatenate([zeros, qt] if upper else [qt, zeros], axis=0)


def _run_band_jobs(jobs, k_refs, vt_refs, s_bufs):
    sizes = [r.shape[0] for r in k_refs]
    offs = [sum(sizes[:i]) for i in range(len(sizes))]
    nblk = len(sizes)

    def qk(job, u, buf):
        nk, off = sizes[u], offs[u]
        s = jnp.dot(k_refs[u][:, job.slot * LANES:(job.slot + 1) * LANES], job.rhs,
                    preferred_element_type=F32) + job.bias(off, nk)
        buf[off:off + nk, :] = s
        return jnp.max(s.reshape(nk // 8, 8, s.shape[1]), axis=0)

    def pv(job, u, buf, m):
        nk, off = sizes[u], offs[u]
        p = jnp.exp(buf[off:off + nk, :] - m).astype(BF16)
        lhs = jnp.concatenate([vt_refs[u][0, job.vrow:job.vrow + HEAD_V, :],
                               jnp.ones((V_ROWS - HEAD_V, nk), BF16)], axis=0)
        return jnp.dot(lhs, p, preferred_element_type=F32)

    part = qk(jobs[0], 0, s_bufs[0])
    for u in range(1, nblk):
        part = jnp.maximum(part, qk(jobs[0], u, s_bufs[0]))
    for i, job in enumerate(jobs):
        nxt = jobs[i + 1] if i + 1 < len(jobs) else None
        cur_buf, nxt_buf = s_bufs[i % 2], s_bufs[(i + 1) % 2]
        m = jnp.max(part, axis=0, keepdims=True)
        if job.sink is not None:
            m = jnp.maximum(m, job.sink)
        part = qk(nxt, 0, nxt_buf) if nxt is not None else None
        acc = None
        for u in range(nblk):
            if nxt is not None and u + 1 < nblk:
                part = jnp.maximum(part, qk(nxt, u + 1, nxt_buf))
            d = pv(job, u, cur_buf, m)
            acc = d if acc is None else acc + d
        denom = acc[HEAD_V:HEAD_V + 1]
        if job.sink is not None:
            denom = denom + jnp.exp(job.sink - m)
        job.write(acc[:HEAD_V] * (1.0 / denom))


def _band_variant(t, tiles_x):
    return jnp.where(t >= tiles_x, 3, jnp.where(t == 0, 0, jnp.where(t == tiles_x - 1, 2, 1)))


def _swa_attn_kernel(qt_ref, kp, kc, kn, kx, vp, vc, vn, vx, mask_ref, sink_ref, ot_ref, s_a, s_b):
    tq = qt_ref.shape[2]
    grp = SWA_Q_HEADS // SWA_KV_HEADS
    lane = lax.broadcasted_iota(jnp.int32, (1, 2 * tq), 1)
    jobs = []
    for g in range(SWA_KV_HEADS):
        for i in range(grp // 2):
            h0 = g * grp + 2 * i
            qt = jnp.concatenate([qt_ref[0, h0 * HEAD_V:(h0 + 1) * HEAD_V, :],
                                  qt_ref[0, (h0 + 1) * HEAD_V:(h0 + 2) * HEAD_V, :]], axis=1)
            sink = jnp.where(lane < tq, sink_ref[h0], sink_ref[h0 + 1])

            def write(o, h0=h0):
                ot_ref[0, h0 * HEAD_V:(h0 + 1) * HEAD_V, :] = o[:, :tq].astype(BF16)
                ot_ref[0, (h0 + 1) * HEAD_V:(h0 + 2) * HEAD_V, :] = o[:, tq:].astype(BF16)

            jobs.append(_BandJob(_slot_rhs(qt, g % 2 == 1), g // 2, g * HEAD_V,
                                 lambda off, nk: mask_ref[0, off:off + nk, :], sink, write))
    _run_band_jobs(jobs, [kp, kc, kn, kx], [vp, vc, vn, vx], (s_a, s_b))


def _swa_attn(geom, qt, k, vt, sink, need_ctx):
    b, n, l, tpb, tm = geom.b, geom.n, geom.l, geom.tpb, geom.tm
    tq = SWA_BLOCK
    per = tm // tq
    tiles_x = n // tq
    nt = tiles_x + (l // tq if need_ctx else 0)
    dq, dkv = SWA_Q_HEADS * SWA_HEAD_DIM, SWA_KV_HEADS * SWA_HEAD_DIM
    width = 3 * tq + l
    mask = jnp.asarray(np.tile(_swa_mask_t(l), (1, 1, 2)))

    def q_map(bi, t):
        return (bi * tpb + jnp.where(t < tiles_x, 1 + t // per, 0), 0,
                jnp.where(t < tiles_x, t % per, t - tiles_x))

    def band_tile(t, delta):
        return jnp.clip(jnp.minimum(t, tiles_x - 1) + delta, 0, tiles_x - 1)

    k_band = [pl.BlockSpec((tq, dkv), lambda bi, t, dl=dl: (bi * tpb * per + per + band_tile(t, dl), 0))
              for dl in (-1, 0, 1)]
    v_band = [pl.BlockSpec((1, dkv, tq), lambda bi, t, dl=dl: (bi * tpb + 1 + band_tile(t, dl) // per, 0,
                                                                band_tile(t, dl) % per))
              for dl in (-1, 0, 1)]
    in_specs = ([pl.BlockSpec((1, dq, tq), q_map)]
                + k_band + [pl.BlockSpec((l, dkv), lambda bi, t: (bi * tpb, 0))]
                + v_band + [pl.BlockSpec((1, dkv, l), lambda bi, t: (bi * tpb, 0, 0))]
                + [pl.BlockSpec((1, width, 2 * tq), lambda bi, t: (_band_variant(t, tiles_x), 0, 0)),
                   pl.BlockSpec(memory_space=pltpu.SMEM)])
    return pl.pallas_call(
        _swa_attn_kernel,
        grid=(b, nt),
        in_specs=in_specs,
        out_specs=pl.BlockSpec((1, dq, tq), q_map),
        out_shape=jax.ShapeDtypeStruct((geom.tiles, dq, tm), BF16),
        scratch_shapes=[pltpu.VMEM((width, 2 * tq), F32), pltpu.VMEM((width, 2 * tq), F32)],
        compiler_params=_cparams(2),
        name="swa_attn",
    )(qt, k, k, k, k, vt, vt, vt, vt, mask, sink)


def _swa_mask_t(l):
    tq = SWA_BLOCK
    j = np.arange(tq)[:, None]
    i = np.arange(tq)[None, :]
    prev_ok = (SWA_BLOCK + i - j) <= SWA_WINDOW
    next_ok = (SWA_BLOCK + j - i) <= SWA_WINDOW
    cur_ok = np.abs(i - j) <= SWA_WINDOW
    none = np.zeros((tq, tq), bool)
    ctx_ok = np.ones((l, tq), bool)
    variants = [
        np.concatenate([none, cur_ok, next_ok, ctx_ok], axis=0),
        np.concatenate([prev_ok, cur_ok, next_ok, ctx_ok], axis=0),
        np.concatenate([prev_ok, cur_ok, none, ctx_ok], axis=0),
        np.concatenate([none, none, none, ctx_ok], axis=0),
    ]
    return np.where(np.stack(variants), 0.0, NEG_INF).astype(np.float32)


NA_TILE_ROWS = 4
NA_HEADS_PER_STEP = 8


def _na_bias_t(rel_bias, l):
    col = np.arange(GRID_W)
    col_start = np.clip(col - NA_COLS // 2, 0, GRID_W - NA_COLS)
    col_valid = (col[None, :] >= col_start[:, None]) & (col[None, :] < col_start[:, None] + NA_COLS)
    dc_idx = np.clip(col[None, :] - col[:, None] + NA_COLS - 1, 0, 2 * NA_COLS - 2)
    bias_cols = rel_bias.astype(F32)[:, :, dc_idx]
    bias_cols = jnp.where(col_valid, bias_cols, NEG_INF)
    r = NA_TILE_ROWS
    a = np.arange(r)[:, None]
    c = np.arange(3 * r)[None, :]
    dr_idx = np.clip(c - r - a + NA_ROWS - 1, 0, 2 * NA_ROWS - 2)
    tiles = bias_cols[:, dr_idx]
    tiles = tiles.transpose(0, 2, 4, 1, 3)
    half = NA_ROWS // 2
    interior = (c - r >= a - half) & (c - r < a - half + NA_ROWS)
    first = (c >= r) & (c < r + NA_ROWS) & (a >= 0)
    last = (c >= 2 * r - NA_ROWS) & (c < 2 * r) & (a >= 0)
    none = np.zeros_like(interior)
    h = rel_bias.shape[0]
    out = []
    for ok in (first, interior, last, none):
        ok5 = ok.T[None, :, None, :, None]
        t = jnp.where(ok5, tiles, NEG_INF).reshape(h, 3 * r * GRID_W, r * GRID_W)
        out.append(jnp.concatenate([t, jnp.zeros((h, l, r * GRID_W), F32)], axis=1))
    return jnp.stack(out)


def _na_attn_kernel(qt_ref, kp, kc, kn, kx, vp, vc, vn, vx, bias_ref, ot_ref, s_a, s_b):
    jobs = []
    for hh in range(NA_HEADS_PER_STEP):
        rows = slice(hh * HEAD_V, (hh + 1) * HEAD_V)

        def write(o, rows=rows):
            ot_ref[0, rows, :] = o.astype(BF16)

        jobs.append(_BandJob(_slot_rhs(qt_ref[0, rows, :], hh % 2 == 1), hh // 2, hh * HEAD_V,
                             lambda off, nk, hh=hh: bias_ref[0, hh, off:off + nk, :], None, write))
    _run_band_jobs(jobs, [kp, kc, kn, kx], [vp, vc, vn, vx], (s_a, s_b))


def _na_attn(geom, qt, k, vt, bias, need_ctx):
    b, n, l, tpb, tm = geom.b, geom.n, geom.l, geom.tpb, geom.tm
    tiles_x = geom.xtpb
    nt = tiles_x + (1 if need_ctx else 0)
    hb = NA_HEADS_PER_STEP
    hw = hb * NA_HEAD_DIM
    width = 3 * tm + l

    def tile_of(bi, t):
        return bi * tpb + jnp.where(t < tiles_x, 1 + t, 0)

    def band_tile(bi, t, delta):
        return bi * tpb + 1 + jnp.clip(jnp.minimum(t, tiles_x - 1) + delta, 0, tiles_x - 1)

    q_spec = pl.BlockSpec((1, hw, tm), lambda g, t, bi: (tile_of(bi, t), g, 0))
    k_band = [pl.BlockSpec((tm, hw), lambda g, t, bi, dl=dl: (band_tile(bi, t, dl), g)) for dl in (-1, 0, 1)]
    v_band = [pl.BlockSpec((1, hw, tm), lambda g, t, bi, dl=dl: (band_tile(bi, t, dl), g, 0)) for dl in (-1, 0, 1)]
    in_specs = ([q_spec] + k_band + [pl.BlockSpec((l, hw), lambda g, t, bi: (bi * tpb, g))]
                + v_band + [pl.BlockSpec((1, hw, l), lambda g, t, bi: (bi * tpb, g, 0))]
                + [pl.BlockSpec((1, hb, width, tm), lambda g, t, bi: (_band_variant(t, tiles_x), g, 0, 0))])
    return pl.pallas_call(
        _na_attn_kernel,
        grid=(NA_HEADS // hb, nt, b),
        in_specs=in_specs,
        out_specs=q_spec,
        out_shape=jax.ShapeDtypeStruct((geom.tiles, NA_HEADS * NA_HEAD_DIM, tm), BF16),
        scratch_shapes=[pltpu.VMEM((width, tm), F32), pltpu.VMEM((width, tm), F32)],
        compiler_params=_cparams(3),
        name="na_attn",
    )(qt, k, k, k, k, vt, vt, vt, vt, bias)


def _post_kernel(*refs, final):
    if final:
        x_ref, o_ref, mod_ref, wo_ref, g_ref, w1_ref, w2_ref, gout_ref, out_ref = refs
    else:
        x_ref, o_ref, mod_ref, wo_ref, g_ref, w1_ref, w2_ref, out_ref = refs
    d = D_MODEL
    mod = mod_ref[...]
    g1, sh2, sc2, g2 = mod[:, 2 * d:3 * d], mod[:, 3 * d:4 * d], mod[:, 4 * d:5 * d], mod[:, 5 * d:6 * d]
    y = lax.dot_general(o_ref[0], wo_ref[...], (((0,), (0,)), ((), ())), preferred_element_type=F32)
    x1 = x_ref[...] + g1 * y
    h2 = _norm_mod(x1, g_ref[...], sh2, sc2).astype(BF16)
    acc = jnp.zeros(x1.shape, F32)
    for c in range(D_FF // d):
        u = jnp.maximum(jnp.dot(h2, w1_ref[:, c * d:(c + 1) * d], preferred_element_type=F32), 0.0)
        acc = acc + jnp.dot((u * u).astype(BF16), w2_ref[c * d:(c + 1) * d, :], preferred_element_type=F32)
    x2 = x1 + g2 * acc
    if final:
        x2 = _rms(x2) * gout_ref[...]
    out_ref[...] = x2


def _post(geom, layer, xall, ot, mods, w_o, g, w1, w2, g_out):
    final = g_out is not None
    tm = geom.tm
    if final:
        tiles = geom.b * geom.xtpb
        tile_of = lambda t: (t // geom.xtpb) * geom.tpb + 1 + t % geom.xtpb
    else:
        tiles = geom.tiles
        tile_of = lambda t: t
    x_spec = pl.BlockSpec((tm, D_MODEL), lambda t: (tile_of(t), 0))
    o_spec = pl.BlockSpec((1, D_MODEL, tm), lambda t: (tile_of(t), 0, 0))
    in_specs = [x_spec, o_spec, _mod_spec(geom, layer, tile_of),
                _const_spec(w_o.shape), _const_spec((1, D_MODEL)), _const_spec(w1.shape), _const_spec(w2.shape)]
    args = [xall, ot, mods, w_o.astype(BF16), g.reshape(1, D_MODEL), w1.astype(BF16), w2.astype(BF16)]
    if final:
        in_specs.append(_const_spec((1, D_MODEL)))
        args.append(g_out.reshape(1, D_MODEL))
    return pl.pallas_call(
        functools.partial(_post_kernel, final=final),
        grid=(tiles,),
        in_specs=in_specs,
        out_specs=pl.BlockSpec((tm, D_MODEL), lambda t: (t, 0)),
        out_shape=jax.ShapeDtypeStruct((tiles * tm, D_MODEL), F32),
        compiler_params=_cparams(1),
        name="post_mlp",
    )(*args)


def kernel(x, c, ctx, c_ctx, ada_w, ada_b, norm_mix_g, norm_mlp_g, norm_out_g, mlp_w1, mlp_w2,
           mla_w_in, mla_qa_g, mla_w_qb, mla_kva_g, mla_w_kvb, mla_w_o,
           swa_w_qkv, swa_sink, swa_w_o, na_w_qkv, na_rel_bias, na_w_o):
    b, n, d = x.shape
    l = ctx.shape[1]
    assert d == D_MODEL and n % GRID_W == 0 and l % SWA_BLOCK == 0 and l == NA_TILE_ROWS * GRID_W
    geom = _Geom(b, n, l)

    mod_rows = -(-(b + 1) // 8) * 8
    c_rows = jnp.concatenate([c, c_ctx[None, :], jnp.zeros((mod_rows - b - 1, d), F32)], axis=0)
    mods = _ada_mods(c_rows, ada_w, ada_b).reshape(DEPTH, mod_rows, 1, N_MOD * d)

    xall = jnp.concatenate([ctx, x], axis=1).reshape(geom.rows, d)
    for i in range(DEPTH):
        need_ctx = i < DEPTH - 1
        kind, j = i % 3, i // 3
        if kind == 0:
            qt, k, vt = _mla_pre(geom, i, xall, mods, norm_mix_g[i], mla_w_in[j], mla_qa_g[j], mla_w_qb[j],
                                 mla_kva_g[j], mla_w_kvb[j])
            o = _mla_attn(geom, qt, k, vt, need_ctx)
            w_o = mla_w_o[j]
        elif kind == 1:
            qt, k, vt = _swa_pre(geom, i, xall, mods, norm_mix_g[i], swa_w_qkv[j])
            o = _swa_attn(geom, qt, k, vt, swa_sink[j], need_ctx)
            w_o = swa_w_o[j]
        else:
            qt, k, vt = _na_pre(geom, i, xall, mods, norm_mix_g[i], na_w_qkv[j])
            o = _na_attn(geom, qt, k, vt, _na_bias_t(na_rel_bias[j], l), need_ctx)
            w_o = na_w_o[j]
        g_out = norm_out_g if i == DEPTH - 1 else None
        xall = _post(geom, i, xall, o, mods, w_o, norm_mlp_g[i], mlp_w1[i], mlp_w2[i], g_out)
    return xall.reshape(b, n, d)
```

```python
import functools
import math

import numpy as np
import jax
import jax.numpy as jnp
from jax import lax
from jax.experimental import pallas as pl
from jax.experimental.pallas import tpu as pltpu

F32 = jnp.float32
BF16 = jnp.bfloat16

D_MODEL = 1024
D_FF = 4 * D_MODEL
DEPTH = 4
N_MOD = 6
GRID_W = 64
EPS = 1e-6
ROPE_BASE = 10000.0
NEG_INF = -1e30

MLA_HEADS = 16
MLA_Q_RANK = 256
MLA_KV_RANK = 128
MLA_NOPE = 64
MLA_ROPE = 32
MLA_V = 64

SWA_Q_HEADS = 16
SWA_KV_HEADS = 4
SWA_HEAD_DIM = 64
SWA_WINDOW = 128
SWA_BLOCK = 128

NA_HEADS = 16
NA_HEAD_DIM = 64
NA_ROWS = 8
NA_COLS = 16

LANES = 128
HEAD_V = 64
CHUNK_UNROLL = 4
V_ROWS = 80
VMEM_LIMIT = 56 * 1024 * 1024
LOG2E = math.log2(math.e)


def _cparams(n_axes):
    return pltpu.CompilerParams(dimension_semantics=("arbitrary",) * n_axes,
                                vmem_limit_bytes=VMEM_LIMIT)


def _const_spec(shape):
    nd = len(shape)
    return pl.BlockSpec(shape, lambda *_: (0,) * nd, pipeline_mode=pl.Buffered(1))


def _rms(x):
    return x * lax.rsqrt(jnp.mean(x * x, axis=-1, keepdims=True) + EPS)


def _norm_mod(x, g, shift, scale):
    return (_rms(x) * g) * (1.0 + scale) + shift


def _ada_kernel(c_ref, w_ref, b_ref, o_ref):
    c = c_ref[...]
    act = c / (1.0 + jnp.exp(-c))
    o_ref[0] = jnp.dot(act.astype(BF16), w_ref[0], preferred_element_type=F32) + b_ref[0]


def _ada_mods(c_rows, ada_w, ada_b):
    depth, d, n = ada_w.shape
    tn = 1536
    return pl.pallas_call(
        _ada_kernel,
        grid=(depth, n // tn),
        in_specs=[
            pl.BlockSpec((c_rows.shape[0], d), lambda i, j: (0, 0)),
            pl.BlockSpec((1, d, tn), lambda i, j: (i, 0, j)),
            pl.BlockSpec((1, 1, tn), lambda i, j: (i, 0, j)),
        ],
        out_specs=pl.BlockSpec((1, c_rows.shape[0], tn), lambda i, j: (i, 0, j)),
        out_shape=jax.ShapeDtypeStruct((depth, c_rows.shape[0], n), F32),
        compiler_params=_cparams(2),
        name="ada_mods",
    )(c_rows, ada_w.astype(BF16), ada_b.reshape(depth, 1, n))


class _Geom:
    def __init__(self, b, n, l):
        assert n % l == 0
        self.b, self.n, self.l = b, n, l
        self.tm = l
        self.tpb = (n + l) // l
        self.xtpb = n // l
        self.tiles = b * self.tpb
        self.rows = self.tiles * self.tm
        assert self.xtpb % 2 == 0
        self.qblocks = self.xtpb // 2 + 1

    def mod_row(self, t):
        return jnp.where(t % self.tpb == 0, self.b, t // self.tpb)

    def pos_block(self, t):
        return t % self.tpb

    def pair_block(self, t):
        j = t % self.tpb
        block = (t // self.tpb) * self.qblocks + jnp.where(j == 0, self.qblocks - 1, (j - 1) // 2)
        return block, jnp.where(j == 0, 0, (j - 1) % 2)


def _mod_spec(geom, layer, tile_of):
    return pl.BlockSpec((None, None, 1, N_MOD * D_MODEL),
                        lambda t: (layer, geom.mod_row(tile_of(t)), 0, 0))


def _row_spec(geom, width):
    return pl.BlockSpec((geom.tm, width), lambda t: (t, 0))


def _pos_spec(geom):
    return pl.BlockSpec((geom.tm, LANES), lambda t: (geom.pos_block(t), 0))


def _pos_spec_t(geom):
    return pl.BlockSpec((LANES, geom.tm), lambda t: (0, geom.pos_block(t)))


def _rope_cos_sin(n, rot_dim):
    t = jnp.arange(n, dtype=jnp.int32)
    row = (t // GRID_W).astype(F32)
    col = (t % GRID_W).astype(F32)
    n_freq = rot_dim // 4
    inv_freq = ROPE_BASE ** (-jnp.arange(n_freq, dtype=F32) / n_freq)
    ang = jnp.concatenate([row[:, None] * inv_freq, col[:, None] * inv_freq], axis=-1)
    return jnp.cos(ang), jnp.sin(ang)


def _rope_tables(geom, rot_dim, lead, scale):
    c, s = _rope_cos_sin(geom.n, rot_dim)
    half = rot_dim // 2
    pad = LANES - lead - rot_dim
    n, l = geom.n, geom.l
    cos_t = jnp.concatenate([jnp.ones((n, lead), F32), c, c, jnp.zeros((n, pad), F32)], axis=1)
    sin_t = jnp.concatenate([jnp.zeros((n, lead), F32), -s, s, jnp.zeros((n, pad), F32)], axis=1)
    cos_c = jnp.concatenate([jnp.ones((l, lead + 2 * half), F32), jnp.zeros((l, pad), F32)], axis=1)
    sin_c = jnp.zeros((l, LANES), F32)
    return (jnp.concatenate([cos_c, cos_t], axis=0) * scale,
            jnp.concatenate([sin_c, sin_t], axis=0) * scale)


def _ones_slot_vector(n_slots):
    row = np.zeros((n_slots, LANES), np.float32)
    row[:, HEAD_V] = 1.0
    return row.reshape(n_slots * LANES)


def _prep_mla_weights(w_in, w_qb, w_kvb):
    d = w_in.shape[0]
    pe0 = MLA_Q_RANK + MLA_KV_RANK
    ev = pe0 + 2 * np.arange(MLA_ROPE // 2)
    od = ev + 1
    z = jnp.zeros((d, LANES - MLA_ROPE), F32)
    win = jnp.concatenate([w_in[:, :pe0], w_in[:, ev], w_in[:, od], z, w_in[:, od], w_in[:, ev], z], axis=1)

    hd = MLA_NOPE + MLA_ROPE
    wq3 = w_qb.reshape(MLA_Q_RANK, MLA_HEADS, hd)
    nope, pe_e, pe_o = wq3[:, :, :MLA_NOPE], wq3[:, :, MLA_NOPE::2], wq3[:, :, MLA_NOPE + 1::2]
    zpad = jnp.zeros((MLA_Q_RANK, MLA_HEADS, LANES - hd), F32)
    zlead = jnp.zeros((MLA_Q_RANK, MLA_HEADS, MLA_NOPE), F32)
    wqa = jnp.concatenate([nope, pe_e, pe_o, zpad], axis=-1).reshape(MLA_Q_RANK, MLA_HEADS * LANES)
    wqb = jnp.concatenate([zlead, pe_o, pe_e, zpad], axis=-1).reshape(MLA_Q_RANK, MLA_HEADS * LANES)
    wq_t = jnp.concatenate([wqa, wqb], axis=1).T

    wkv3 = w_kvb.reshape(MLA_KV_RANK, MLA_HEADS, MLA_NOPE + MLA_V)
    zk = jnp.zeros((MLA_KV_RANK, MLA_HEADS, LANES - MLA_NOPE), F32)
    wk_top = jnp.concatenate([wkv3[:, :, :MLA_NOPE], zk], axis=-1).reshape(MLA_KV_RANK, MLA_HEADS * LANES)
    eye = np.zeros((LANES, MLA_HEADS, LANES), np.float32)
    for j in range(MLA_ROPE):
        eye[j, :, MLA_NOPE + j] = 1.0
    wk = jnp.concatenate([wk_top, jnp.asarray(eye.reshape(LANES, MLA_HEADS * LANES))], axis=0)
    zv = jnp.zeros((MLA_KV_RANK, MLA_HEADS, LANES - MLA_V), F32)
    wv_t = jnp.concatenate([wkv3[:, :, MLA_NOPE:], zv], axis=-1).reshape(MLA_KV_RANK, MLA_HEADS * LANES).T
    return win.astype(BF16), wq_t.astype(BF16), wk.astype(BF16), wv_t.astype(BF16)


def _mla_pre_kernel(x_ref, mod_ref, g_ref, win_ref, qag_ref, kvag_ref, wqt_ref, wk_ref, wvt_ref,
                    cqt_ref, sqt_ref, ck_ref, sk_ref, ones_ref, qt_out, k_out, vt_out):
    nt = (((1,), (1,)), ((), ()))
    mod = mod_ref[...]
    h = _norm_mod(x_ref[...], g_ref[...], mod[:, :D_MODEL], mod[:, D_MODEL:2 * D_MODEL]).astype(BF16)
    proj = jnp.dot(h, win_ref[...], preferred_element_type=F32)
    kv0 = MLA_Q_RANK
    pe0 = kv0 + MLA_KV_RANK
    qn = (_rms(proj[:, :kv0]) * qag_ref[...]).astype(BF16)
    kvn = _rms(proj[:, kv0:pe0]) * kvag_ref[...]
    qab_t = lax.dot_general(wqt_ref[...], qn, nt, preferred_element_type=F32)
    cqt, sqt = cqt_ref[...], sqt_ref[...]
    half = MLA_HEADS * LANES
    for hh in range(MLA_HEADS):
        lo = hh * LANES
        qt_out[0, lo:lo + LANES, :] = (qab_t[lo:lo + LANES] * cqt
                                       + qab_t[half + lo:half + lo + LANES] * sqt).astype(BF16)
    kpe = proj[:, pe0:pe0 + LANES] * ck_ref[...] + proj[:, pe0 + LANES:pe0 + 2 * LANES] * sk_ref[...]
    lat = jnp.concatenate([kvn, kpe], axis=1).astype(BF16)
    k_out[...] = jnp.dot(lat, wk_ref[...], preferred_element_type=F32).astype(BF16)
    vt = lax.dot_general(wvt_ref[...], kvn.astype(BF16), nt, preferred_element_type=F32)
    vt_out[0] = (vt + ones_ref[...]).astype(BF16)


def _mla_pre(geom, layer, xall, mods, g, w_in, qa_g, w_qb, kva_g, w_kvb):
    win, wq_t, wk, wv_t = _prep_mla_weights(w_in, w_qb, w_kvb)
    scale = (MLA_NOPE + MLA_ROPE) ** -0.5 * LOG2E
    cq, sq = _rope_tables(geom, MLA_ROPE, MLA_NOPE, scale)
    ck, sk = _rope_tables(geom, MLA_ROPE, 0, 1.0)
    width = MLA_HEADS * LANES
    tm = geom.tm
    t_spec = pl.BlockSpec((1, width, tm), lambda t: (t, 0, 0))
    t_shape = jax.ShapeDtypeStruct((geom.tiles, width, tm), BF16)
    q_spec = pl.BlockSpec((1, width, tm), lambda t: (geom.pair_block(t)[0], 0, geom.pair_block(t)[1]))
    q_shape = jax.ShapeDtypeStruct((geom.b * geom.qblocks, width, 2 * tm), BF16)
    return pl.pallas_call(
        _mla_pre_kernel,
        grid=(geom.tiles,),
        in_specs=[
            _row_spec(geom, D_MODEL), _mod_spec(geom, layer, lambda t: t), _const_spec((1, D_MODEL)),
            _const_spec(win.shape), _const_spec((1, MLA_Q_RANK)), _const_spec((1, MLA_KV_RANK)),
            _const_spec(wq_t.shape), _const_spec(wk.shape), _const_spec(wv_t.shape),
            _pos_spec_t(geom), _pos_spec_t(geom), _pos_spec(geom), _pos_spec(geom),
            _const_spec((width, 1)),
        ],
        out_specs=[q_spec, _row_spec(geom, width), t_spec],
        out_shape=[q_shape, jax.ShapeDtypeStruct((geom.rows, width), BF16), t_shape],
        compiler_params=_cparams(1),
        name="mla_pre",
    )(xall, mods, g.reshape(1, D_MODEL), win, qa_g.reshape(1, -1), kva_g.reshape(1, -1), wq_t, wk, wv_t,
      cq.T, sq.T, ck, sk, jnp.asarray(_ones_slot_vector(MLA_HEADS).reshape(width, 1)))


def _mla_attn_kernel(qt_ref, k_ref, vt_ref, ot_ref, s_a, s_b, *, cs, n_chunks, n_latent, with_ctx):
    tm = vt_ref.shape[2]
    tq = qt_ref.shape[2]
    t = pl.program_id(2)

    def normalised(accs):
        return jnp.concatenate([acc[:HEAD_V] * (1.0 / acc[HEAD_V:HEAD_V + 1]) for acc in accs], axis=0)

    def _context_tile():
        accs = []
        for j in range(2):
            s = jnp.dot(k_ref[0:tm, j * LANES:(j + 1) * LANES], qt_ref[0, j * LANES:(j + 1) * LANES, 0:tm],
                        preferred_element_type=F32)
            p = jnp.exp2(s - jnp.max(s, axis=0, keepdims=True)).astype(BF16)
            accs.append(jnp.dot(vt_ref[0, j * LANES:j * LANES + V_ROWS, :], p, preferred_element_type=F32))
        ot_ref[0, :, 0:tm] = normalised(accs).astype(BF16)
        ot_ref[0, :, tm:] = jnp.zeros((2 * HEAD_V, tq - tm), BF16)

    def _latent_tile():
        qts = [qt_ref[0, j * LANES:(j + 1) * LANES, :] for j in range(2)]

        def qk(c, j, u, s_ref):
            r0 = pl.multiple_of((c * cs + u) * tm, tm)
            s = jnp.dot(k_ref[pl.ds(r0, tm), j * LANES:(j + 1) * LANES], qts[j], preferred_element_type=F32)
            s_ref[j, u * tm:(u + 1) * tm, :] = s
            return jnp.max(s.reshape(tm // 8, 8, tq), axis=0)

        def pv(c, j, u, s_ref, m_new):
            p = jnp.exp2(s_ref[j, u * tm:(u + 1) * tm, :] - m_new).astype(BF16)
            return jnp.dot(vt_ref[c * cs + u, j * LANES:j * LANES + V_ROWS, :], p, preferred_element_type=F32)

        def first_scores(s_ref):
            maxes = []
            for j in range(2):
                part = qk(0, j, 0, s_ref)
                for u in range(1, cs):
                    part = jnp.maximum(part, qk(0, j, u, s_ref))
                maxes.append(jnp.max(part, axis=0, keepdims=True))
            return tuple(maxes)

        def step(c, s_cur, s_next, maxes, ms, accs, with_next):
            new_ms, new_accs, next_maxes = [], [], []
            for j in range(2):
                m_new = jnp.maximum(ms[j], maxes[j])
                alpha = jnp.exp2(ms[j] - m_new)
                part = qk(c + 1, j, 0, s_next) if with_next else None
                acc_new = None
                for u in range(cs):
                    if with_next and u + 1 < cs:
                        part = jnp.maximum(part, qk(c + 1, j, u + 1, s_next))
                    d = pv(c, j, u, s_cur, m_new)
                    acc_new = d if acc_new is None else acc_new + d
                new_accs.append(alpha * accs[j] + acc_new)
                new_ms.append(m_new)
                if with_next:
                    next_maxes.append(jnp.max(part, axis=0, keepdims=True))
            return tuple(next_maxes), tuple(new_ms), tuple(new_accs)

        ms = tuple(jnp.full((1, tq), NEG_INF, F32) for _ in range(2))
        accs = tuple(jnp.zeros((V_ROWS, tq), F32) for _ in range(2))
        bufs = (s_a, s_b)
        maxes = first_scores(s_a)
        n_loop = (n_chunks - 1) // CHUNK_UNROLL

        def body(i, carry):
            maxes, ms, accs = carry
            for r in range(CHUNK_UNROLL):
                maxes, ms, accs = step(CHUNK_UNROLL * i + r, bufs[r % 2], bufs[(r + 1) % 2], maxes, ms, accs, True)
            return maxes, ms, accs

        maxes, ms, accs = lax.fori_loop(0, n_loop, body, (maxes, ms, accs))
        for c in range(n_loop * CHUNK_UNROLL, n_chunks):
            maxes, ms, accs = step(c, bufs[c % 2], bufs[(c + 1) % 2], maxes, ms, accs, c + 1 < n_chunks)
        ot_ref[0] = normalised(accs).astype(BF16)

    if with_ctx:
        pl.when(t == n_latent)(_context_tile)
        pl.when(t < n_latent)(_latent_tile)
    else:
        _latent_tile()


def _mla_attn(geom, qt, k, vt, need_ctx):
    b, tpb, tm = geom.b, geom.tpb, geom.tm
    cs = 3 if tpb % 3 == 0 else (2 if tpb % 2 == 0 else 1)
    n_latent = geom.qblocks - 1
    tq = 2 * tm

    def q_map(bi, hp, t):
        return (bi * geom.qblocks + t, hp, 0)

    return pl.pallas_call(
        functools.partial(_mla_attn_kernel, cs=cs, n_chunks=tpb // cs, n_latent=n_latent, with_ctx=need_ctx),
        grid=(b, MLA_HEADS // 2, n_latent + (1 if need_ctx else 0)),
        in_specs=[pl.BlockSpec((1, 2 * LANES, tq), q_map),
                  pl.BlockSpec((tpb * tm, 2 * LANES), lambda bi, hp, t: (bi, hp)),
                  pl.BlockSpec((tpb, 2 * LANES, tm), lambda bi, hp, t: (bi, hp, 0))],
        out_specs=pl.BlockSpec((1, 2 * HEAD_V, tq), q_map),
        out_shape=jax.ShapeDtypeStruct((b * geom.qblocks, MLA_HEADS * HEAD_V, tq), BF16),
        scratch_shapes=[pltpu.VMEM((2, cs * tm, tq), F32), pltpu.VMEM((2, cs * tm, tq), F32)],
        compiler_params=_cparams(3),
        name="mla_attn",
    )(qt, k, vt)


def _head_rope_tables(geom, scale):
    c, s = _rope_cos_sin(geom.n, SWA_HEAD_DIM)
    cos_t = jnp.concatenate([jnp.ones((geom.l, SWA_HEAD_DIM), F32), jnp.concatenate([c, c], axis=1)], axis=0)
    sin_t = jnp.concatenate([jnp.zeros((geom.l, SWA_HEAD_DIM), F32), jnp.concatenate([-s, s], axis=1)], axis=0)
    return cos_t * scale, sin_t * scale


def _prep_swa_weights(w_qkv):
    d = w_qkv.shape[0]
    dq = SWA_Q_HEADS * SWA_HEAD_DIM
    dkv = SWA_KV_HEADS * SWA_HEAD_DIM

    def rope_pair(w, heads):
        w3 = w.reshape(d, heads, SWA_HEAD_DIM)
        e, o = w3[:, :, 0::2], w3[:, :, 1::2]
        return (jnp.concatenate([e, o], axis=-1).reshape(d, heads * SWA_HEAD_DIM),
                jnp.concatenate([o, e], axis=-1).reshape(d, heads * SWA_HEAD_DIM))

    wqa, wqb = rope_pair(w_qkv[:, :dq], SWA_Q_HEADS)
    wka, wkb = rope_pair(w_qkv[:, dq:dq + dkv], SWA_KV_HEADS)
    w_t = jnp.concatenate([wqa, wqb, w_qkv[:, dq + dkv:]], axis=1).T
    return w_t.astype(BF16), jnp.concatenate([wka, wkb], axis=1).astype(BF16)


def _swa_pre_kernel(x_ref, mod_ref, g_ref, wt_ref, wk_ref, cqt_ref, sqt_ref, ck_ref, sk_ref,
                    qt_out, k_out, vt_out):
    nt = (((1,), (1,)), ((), ()))
    mod = mod_ref[...]
    h = _norm_mod(x_ref[...], g_ref[...], mod[:, :D_MODEL], mod[:, D_MODEL:2 * D_MODEL]).astype(BF16)
    tm = h.shape[0]
    dq = SWA_Q_HEADS * SWA_HEAD_DIM
    dkv = SWA_KV_HEADS * SWA_HEAD_DIM
    r = lax.dot_general(wt_ref[...], h, nt, preferred_element_type=F32)
    qa = r[:dq].reshape(SWA_Q_HEADS, SWA_HEAD_DIM, tm)
    qb = r[dq:2 * dq].reshape(SWA_Q_HEADS, SWA_HEAD_DIM, tm)
    qt_out[0] = (qa * cqt_ref[...][None] + qb * sqt_ref[...][None]).reshape(dq, tm).astype(BF16)
    vt_out[0] = r[2 * dq:].astype(BF16)
    kab = jnp.dot(h, wk_ref[...], preferred_element_type=F32)
    k_out[...] = (kab[:, :dkv] * ck_ref[...] + kab[:, dkv:] * sk_ref[...]).astype(BF16)


def _swa_pre(geom, layer, xall, mods, g, w_qkv):
    w_t, wk = _prep_swa_weights(w_qkv)
    cq, sq = _head_rope_tables(geom, SWA_HEAD_DIM ** -0.5 * LOG2E)
    ck, sk = _head_rope_tables(geom, 1.0)
    ck, sk = jnp.tile(ck, (1, SWA_KV_HEADS)), jnp.tile(sk, (1, SWA_KV_HEADS))
    tm = geom.tm
    dq, dkv = SWA_Q_HEADS * SWA_HEAD_DIM, SWA_KV_HEADS * SWA_HEAD_DIM
    qt_pos = pl.BlockSpec((SWA_HEAD_DIM, tm), lambda t: (0, geom.pos_block(t)))
    k_pos = pl.BlockSpec((tm, dkv), lambda t: (geom.pos_block(t), 0))
    return pl.pallas_call(
        _swa_pre_kernel,
        grid=(geom.tiles,),
        in_specs=[
            _row_spec(geom, D_MODEL), _mod_spec(geom, layer, lambda t: t), _const_spec((1, D_MODEL)),
            _const_spec(w_t.shape), _const_spec(wk.shape), qt_pos, qt_pos, k_pos, k_pos,
        ],
        out_specs=[pl.BlockSpec((1, dq, tm), lambda t: (t, 0, 0)), _row_spec(geom, dkv),
                   pl.BlockSpec((1, dkv, tm), lambda t: (t, 0, 0))],
        out_shape=[jax.ShapeDtypeStruct((geom.tiles, dq, tm), BF16),
                   jax.ShapeDtypeStruct((geom.rows, dkv), BF16),
                   jax.ShapeDtypeStruct((geom.tiles, dkv, tm), BF16)],
        compiler_params=_cparams(1),
        name="swa_pre",
    )(xall, mods, g.reshape(1, D_MODEL), w_t, wk, cq.T, sq.T, ck, sk)


def _na_pre_kernel(x_ref, mod_ref, g_ref, wt_ref, wk_ref, qt_out, k_out, vt_out):
    nt = (((1,), (1,)), ((), ()))
    mod = mod_ref[...]
    h = _norm_mod(x_ref[...], g_ref[...], mod[:, :D_MODEL], mod[:, D_MODEL:2 * D_MODEL]).astype(BF16)
    dm = NA_HEADS * NA_HEAD_DIM
    r = lax.dot_general(wt_ref[...], h, nt, preferred_element_type=F32)
    qt_out[0] = r[:dm].astype(BF16)
    vt_out[0] = r[dm:].astype(BF16)
    k_out[...] = jnp.dot(h, wk_ref[...], preferred_element_type=F32).astype(BF16)


def _na_pre(geom, layer, xall, mods, g, w_qkv):
    dm = NA_HEADS * NA_HEAD_DIM
    scale = NA_HEAD_DIM ** -0.5 * LOG2E
    w_t = jnp.concatenate([w_qkv[:, :dm] * scale, w_qkv[:, 2 * dm:]], axis=1).T.astype(BF16)
    wk = w_qkv[:, dm:2 * dm].astype(BF16)
    tm = geom.tm
    t_spec = pl.BlockSpec((1, dm, tm), lambda t: (t, 0, 0))
    t_shape = jax.ShapeDtypeStruct((geom.tiles, dm, tm), BF16)
    return pl.pallas_call(
        _na_pre_kernel,
        grid=(geom.tiles,),
        in_specs=[_row_spec(geom, D_MODEL), _mod_spec(geom, layer, lambda t: t), _const_spec((1, D_MODEL)),
                  _const_spec(w_t.shape), _const_spec(wk.shape)],
        out_specs=[t_spec, _row_spec(geom, dm), t_spec],
        out_shape=[t_shape, jax.ShapeDtypeStruct((geom.rows, dm), BF16), t_shape],
        compiler_params=_cparams(1),
        name="na_pre",
    )(xall, mods, g.reshape(1, D_MODEL), w_t, wk)


class _BandJob:
    def __init__(self, rhs, slot, vrow, bias, sink, write):
        self.rhs, self.slot, self.vrow, self.bias, self.sink, self.write = rhs, slot, vrow, bias, sink, write


def _slot_rhs(qt, upper):
    zeros = jnp.zeros_like(qt)
    return jnp.concatenate([zeros, qt] if upper else [qt, zeros], axis=0)


def _run_band_jobs(jobs, k_refs, vt_refs, s_bufs):
    sizes = [r.shape[0] for r in k_refs]
    offs = [sum(sizes[:i]) for i in range(len(sizes))]
    nblk = len(sizes)

    def qk(job, u, buf):
        nk, off = sizes[u], offs[u]
        s = jnp.dot(k_refs[u][:, job.slot * LANES:(job.slot + 1) * LANES], job.rhs,
                    preferred_element_type=F32) + job.bias(off, nk)
        buf[off:off + nk, :] = s
        return jnp.max(s.reshape(nk // 8, 8, s.shape[1]), axis=0)

    def pv(job, u, buf, m):
        nk, off = sizes[u], offs[u]
        p = jnp.exp2(buf[off:off + nk, :] - m).astype(BF16)
        lhs = jnp.concatenate([vt_refs[u][0, job.vrow:job.vrow + HEAD_V, :],
                               jnp.ones((V_ROWS - HEAD_V, nk), BF16)], axis=0)
        return jnp.dot(lhs, p, preferred_element_type=F32)

    part = qk(jobs[0], 0, s_bufs[0])
    for u in range(1, nblk):
        part = jnp.maximum(part, qk(jobs[0], u, s_bufs[0]))
    for i, job in enumerate(jobs):
        nxt = jobs[i + 1] if i + 1 < len(jobs) else None
        cur_buf, nxt_buf = s_bufs[i % 2], s_bufs[(i + 1) % 2]
        m = jnp.max(part, axis=0, keepdims=True)
        if job.sink is not None:
            m = jnp.maximum(m, job.sink)
        part = qk(nxt, 0, nxt_buf) if nxt is not None else None
        acc = None
        for u in range(nblk):
            if nxt is not None and u + 1 < nblk:
                part = jnp.maximum(part, qk(nxt, u + 1, nxt_buf))
            d = pv(job, u, cur_buf, m)
            acc = d if acc is None else acc + d
        denom = acc[HEAD_V:HEAD_V + 1]
        if job.sink is not None:
            denom = denom + jnp.exp2(job.sink - m)
        job.write(acc[:HEAD_V] * (1.0 / denom))


def _band_variant(t, tiles_x):
    return jnp.where(t >= tiles_x, 3, jnp.where(t == 0, 0, jnp.where(t == tiles_x - 1, 2, 1)))


def _swa_attn_kernel(qt_ref, kp, kc, kn, kx, vp, vc, vn, vx, mask_ref, sink_ref, ot_ref, s_a, s_b):
    tq = qt_ref.shape[2]
    grp = SWA_Q_HEADS // SWA_KV_HEADS
    lane = lax.broadcasted_iota(jnp.int32, (1, 2 * tq), 1)
    jobs = []
    for g in range(SWA_KV_HEADS):
        for i in range(grp // 2):
            h0 = g * grp + 2 * i
            qt = jnp.concatenate([qt_ref[0, h0 * HEAD_V:(h0 + 1) * HEAD_V, :],
                                  qt_ref[0, (h0 + 1) * HEAD_V:(h0 + 2) * HEAD_V, :]], axis=1)
            sink = jnp.where(lane < tq, sink_ref[h0], sink_ref[h0 + 1]) * LOG2E

            def write(o, h0=h0):
                ot_ref[0, h0 * HEAD_V:(h0 + 1) * HEAD_V, :] = o[:, :tq].astype(BF16)
                ot_ref[0, (h0 + 1) * HEAD_V:(h0 + 2) * HEAD_V, :] = o[:, tq:].astype(BF16)

            jobs.append(_BandJob(_slot_rhs(qt, g % 2 == 1), g // 2, g * HEAD_V,
                                 lambda off, nk: mask_ref[0, off:off + nk, :], sink, write))
    _run_band_jobs(jobs, [kp, kc, kn, kx], [vp, vc, vn, vx], (s_a, s_b))


def _swa_attn(geom, qt, k, vt, sink, need_ctx):
    b, n, l, tpb, tm = geom.b, geom.n, geom.l, geom.tpb, geom.tm
    tq = SWA_BLOCK
    per = tm // tq
    tiles_x = n // tq
    nt = tiles_x + (l // tq if need_ctx else 0)
    dq, dkv = SWA_Q_HEADS * SWA_HEAD_DIM, SWA_KV_HEADS * SWA_HEAD_DIM
    width = 3 * tq + l
    mask = jnp.asarray(np.tile(_swa_mask_t(l), (1, 1, 2)))

    def q_map(bi, t):
        return (bi * tpb + jnp.where(t < tiles_x, 1 + t // per, 0), 0,
                jnp.where(t < tiles_x, t % per, t - tiles_x))

    def band_tile(t, delta):
        return jnp.clip(jnp.minimum(t, tiles_x - 1) + delta, 0, tiles_x - 1)

    k_band = [pl.BlockSpec((tq, dkv), lambda bi, t, dl=dl: (bi * tpb * per + per + band_tile(t, dl), 0))
              for dl in (-1, 0, 1)]
    v_band = [pl.BlockSpec((1, dkv, tq), lambda bi, t, dl=dl: (bi * tpb + 1 + band_tile(t, dl) // per, 0,
                                                                band_tile(t, dl) % per))
              for dl in (-1, 0, 1)]
    in_specs = ([pl.BlockSpec((1, dq, tq), q_map)]
                + k_band + [pl.BlockSpec((l, dkv), lambda bi, t: (bi * tpb, 0))]
                + v_band + [pl.BlockSpec((1, dkv, l), lambda bi, t: (bi * tpb, 0, 0))]
                + [pl.BlockSpec((1, width, 2 * tq), lambda bi, t: (_band_variant(t, tiles_x), 0, 0)),
                   pl.BlockSpec(memory_space=pltpu.SMEM)])
    return pl.pallas_call(
        _swa_attn_kernel,
        grid=(b, nt),
        in_specs=in_specs,
        out_specs=pl.BlockSpec((1, dq, tq), q_map),
        out_shape=jax.ShapeDtypeStruct((geom.tiles, dq, tm), BF16),
        scratch_shapes=[pltpu.VMEM((width, 2 * tq), F32), pltpu.VMEM((width, 2 * tq), F32)],
        compiler_params=_cparams(2),
        name="swa_attn",
    )(qt, k, k, k, k, vt, vt, vt, vt, mask, sink)


def _swa_mask_t(l):
    tq = SWA_BLOCK
    j = np.arange(tq)[:, None]
    i = np.arange(tq)[None, :]
    prev_ok = (SWA_BLOCK + i - j) <= SWA_WINDOW
    next_ok = (SWA_BLOCK + j - i) <= SWA_WINDOW
    cur_ok = np.abs(i - j) <= SWA_WINDOW
    none = np.zeros((tq, tq), bool)
    ctx_ok = np.ones((l, tq), bool)
    variants = [
        np.concatenate([none, cur_ok, next_ok, ctx_ok], axis=0),
        np.concatenate([prev_ok, cur_ok, next_ok, ctx_ok], axis=0),
        np.concatenate([prev_ok, cur_ok, none, ctx_ok], axis=0),
        np.concatenate([none, none, none, ctx_ok], axis=0),
    ]
    return np.where(np.stack(variants), 0.0, NEG_INF).astype(np.float32)


NA_TILE_ROWS = 4
NA_HEADS_PER_STEP = 8


def _na_bias_t(rel_bias, l):
    col = np.arange(GRID_W)
    col_start = np.clip(col - NA_COLS // 2, 0, GRID_W - NA_COLS)
    col_valid = (col[None, :] >= col_start[:, None]) & (col[None, :] < col_start[:, None] + NA_COLS)
    dc_idx = np.clip(col[None, :] - col[:, None] + NA_COLS - 1, 0, 2 * NA_COLS - 2)
    bias_cols = (rel_bias.astype(F32) * LOG2E)[:, :, dc_idx]
    bias_cols = jnp.where(col_valid, bias_cols, NEG_INF)
    r = NA_TILE_ROWS
    a = np.arange(r)[:, None]
    c = np.arange(3 * r)[None, :]
    dr_idx = np.clip(c - r - a + NA_ROWS - 1, 0, 2 * NA_ROWS - 2)
    tiles = bias_cols[:, dr_idx]
    tiles = tiles.transpose(0, 2, 4, 1, 3)
    half = NA_ROWS // 2
    interior = (c - r >= a - half) & (c - r < a - half + NA_ROWS)
    first = (c >= r) & (c < r + NA_ROWS) & (a >= 0)
    last = (c >= 2 * r - NA_ROWS) & (c < 2 * r) & (a >= 0)
    none = np.zeros_like(interior)
    h = rel_bias.shape[0]
    out = []
    for ok in (first, interior, last, none):
        ok5 = ok.T[None, :, None, :, None]
        t = jnp.where(ok5, tiles, NEG_INF).reshape(h, 3 * r * GRID_W, r * GRID_W)
        out.append(jnp.concatenate([t, jnp.zeros((h, l, r * GRID_W), F32)], axis=1))
    return jnp.stack(out)


def _na_attn_kernel(qt_ref, kp, kc, kn, kx, vp, vc, vn, vx, bias_ref, ot_ref, s_a, s_b):
    jobs = []
    for hh in range(NA_HEADS_PER_STEP):
        rows = slice(hh * HEAD_V, (hh + 1) * HEAD_V)

        def write(o, rows=rows):
            ot_ref[0, rows, :] = o.astype(BF16)

        jobs.append(_BandJob(_slot_rhs(qt_ref[0, rows, :], hh % 2 == 1), hh // 2, hh * HEAD_V,
                             lambda off, nk, hh=hh: bias_ref[0, hh, off:off + nk, :], None, write))
    _run_band_jobs(jobs, [kp, kc, kn, kx], [vp, vc, vn, vx], (s_a, s_b))


def _na_attn(geom, qt, k, vt, bias, need_ctx):
    b, n, l, tpb, tm = geom.b, geom.n, geom.l, geom.tpb, geom.tm
    tiles_x = geom.xtpb
    nt = tiles_x + (1 if need_ctx else 0)
    hb = NA_HEADS_PER_STEP
    hw = hb * NA_HEAD_DIM
    width = 3 * tm + l

    def tile_of(bi, t):
        return bi * tpb + jnp.where(t < tiles_x, 1 + t, 0)

    def band_tile(bi, t, delta):
        return bi * tpb + 1 + jnp.clip(jnp.minimum(t, tiles_x - 1) + delta, 0, tiles_x - 1)

    q_spec = pl.BlockSpec((1, hw, tm), lambda g, t, bi: (tile_of(bi, t), g, 0))
    k_band = [pl.BlockSpec((tm, hw), lambda g, t, bi, dl=dl: (band_tile(bi, t, dl), g)) for dl in (-1, 0, 1)]
    v_band = [pl.BlockSpec((1, hw, tm), lambda g, t, bi, dl=dl: (band_tile(bi, t, dl), g, 0)) for dl in (-1, 0, 1)]
    in_specs = ([q_spec] + k_band + [pl.BlockSpec((l, hw), lambda g, t, bi: (bi * tpb, g))]
                + v_band + [pl.BlockSpec((1, hw, l), lambda g, t, bi: (bi * tpb, g, 0))]
                + [pl.BlockSpec((1, hb, width, tm), lambda g, t, bi: (_band_variant(t, tiles_x), g, 0, 0))])
    return pl.pallas_call(
        _na_attn_kernel,
        grid=(NA_HEADS // hb, nt, b),
        in_specs=in_specs,
        out_specs=q_spec,
        out_shape=jax.ShapeDtypeStruct((geom.tiles, NA_HEADS * NA_HEAD_DIM, tm), BF16),
        scratch_shapes=[pltpu.VMEM((width, tm), F32), pltpu.VMEM((width, tm), F32)],
        compiler_params=_cparams(3),
        name="na_attn",
    )(qt, k, k, k, k, vt, vt, vt, vt, bias)


def _post_kernel(*refs, final):
    if final:
        x_ref, o_ref, mod_ref, wo_ref, g_ref, w1_ref, w2_ref, gout_ref, out_ref = refs
    else:
        x_ref, o_ref, mod_ref, wo_ref, g_ref, w1_ref, w2_ref, out_ref = refs
    d = D_MODEL
    mod = mod_ref[...]
    g1, sh2, sc2, g2 = mod[:, 2 * d:3 * d], mod[:, 3 * d:4 * d], mod[:, 4 * d:5 * d], mod[:, 5 * d:6 * d]
    y = lax.dot_general(o_ref[0], wo_ref[...], (((0,), (0,)), ((), ())), preferred_element_type=F32)
    x1 = x_ref[...] + g1 * y
    h2 = _norm_mod(x1, g_ref[...], sh2, sc2).astype(BF16)
    acc = jnp.zeros(x1.shape, F32)
    for c in range(D_FF // d):
        u = jnp.maximum(jnp.dot(h2, w1_ref[:, c * d:(c + 1) * d], preferred_element_type=F32), 0.0)
        acc = acc + jnp.dot((u * u).astype(BF16), w2_ref[c * d:(c + 1) * d, :], preferred_element_type=F32)
    x2 = x1 + g2 * acc
    if final:
        x2 = _rms(x2) * gout_ref[...]
    out_ref[...] = x2


def _post(geom, layer, xall, ot, ot_paired, mods, w_o, g, w1, w2, g_out):
    final = g_out is not None
    tm = geom.tm
    if final:
        tiles = geom.b * geom.xtpb
        tile_of = lambda t: (t // geom.xtpb) * geom.tpb + 1 + t % geom.xtpb
    else:
        tiles = geom.tiles
        tile_of = lambda t: t
    x_spec = pl.BlockSpec((tm, D_MODEL), lambda t: (tile_of(t), 0))
    if ot_paired:
        o_spec = pl.BlockSpec((1, D_MODEL, tm),
                              lambda t: (geom.pair_block(tile_of(t))[0], 0, geom.pair_block(tile_of(t))[1]))
    else:
        o_spec = pl.BlockSpec((1, D_MODEL, tm), lambda t: (tile_of(t), 0, 0))
    in_specs = [x_spec, o_spec, _mod_spec(geom, layer, tile_of),
                _const_spec(w_o.shape), _const_spec((1, D_MODEL)), _const_spec(w1.shape), _const_spec(w2.shape)]
    args = [xall, ot, mods, w_o.astype(BF16), g.reshape(1, D_MODEL), w1.astype(BF16), w2.astype(BF16)]
    if final:
        in_specs.append(_const_spec((1, D_MODEL)))
        args.append(g_out.reshape(1, D_MODEL))
    return pl.pallas_call(
        functools.partial(_post_kernel, final=final),
        grid=(tiles,),
        in_specs=in_specs,
        out_specs=pl.BlockSpec((tm, D_MODEL), lambda t: (t, 0)),
        out_shape=jax.ShapeDtypeStruct((tiles * tm, D_MODEL), F32),
        compiler_params=_cparams(1),
        name="post_mlp",
    )(*args)


def kernel(x, c, ctx, c_ctx, ada_w, ada_b, norm_mix_g, norm_mlp_g, norm_out_g, mlp_w1, mlp_w2,
           mla_w_in, mla_qa_g, mla_w_qb, mla_kva_g, mla_w_kvb, mla_w_o,
           swa_w_qkv, swa_sink, swa_w_o, na_w_qkv, na_rel_bias, na_w_o):
    b, n, d = x.shape
    l = ctx.shape[1]
    assert d == D_MODEL and n % GRID_W == 0 and l % SWA_BLOCK == 0 and l == NA_TILE_ROWS * GRID_W
    geom = _Geom(b, n, l)

    mod_rows = -(-(b + 1) // 8) * 8
    c_rows = jnp.concatenate([c, c_ctx[None, :], jnp.zeros((mod_rows - b - 1, d), F32)], axis=0)
    mods = _ada_mods(c_rows, ada_w, ada_b).reshape(DEPTH, mod_rows, 1, N_MOD * d)

    xall = jnp.concatenate([ctx, x], axis=1).reshape(geom.rows, d)
    for i in range(DEPTH):
        need_ctx = i < DEPTH - 1
        kind, j = i % 3, i // 3
        if kind == 0:
            qt, k, vt = _mla_pre(geom, i, xall, mods, norm_mix_g[i], mla_w_in[j], mla_qa_g[j], mla_w_qb[j],
                                 mla_kva_g[j], mla_w_kvb[j])
            o = _mla_attn(geom, qt, k, vt, need_ctx)
            w_o = mla_w_o[j]
        elif kind == 1:
            qt, k, vt = _swa_pre(geom, i, xall, mods, norm_mix_g[i], swa_w_qkv[j])
            o = _swa_attn(geom, qt, k, vt, swa_sink[j], need_ctx)
            w_o = swa_w_o[j]
        else:
            qt, k, vt = _na_pre(geom, i, xall, mods, norm_mix_g[i], na_w_qkv[j])
            o = _na_attn(geom, qt, k, vt, _na_bias_t(na_rel_bias[j], l), need_ctx)
            w_o = na_w_o[j]
        g_out = norm_out_g if i == DEPTH - 1 else None
        xall = _post(geom, i, xall, o, kind == 0, mods, w_o, norm_mlp_g[i], mlp_w1[i], mlp_w2[i], g_out)
    return xall.reshape(b, n, d)
```

```python
import functools
import math

import numpy as np
import jax
import jax.numpy as jnp
from jax import lax
from jax.experimental import pallas as pl
from jax.experimental.pallas import tpu as pltpu

F32 = jnp.float32
BF16 = jnp.bfloat16

D_MODEL = 1024
D_FF = 4 * D_MODEL
DEPTH = 4
N_MOD = 6
GRID_W = 64
EPS = 1e-6
ROPE_BASE = 10000.0
NEG_INF = -1e30

MLA_HEADS = 16
MLA_Q_RANK = 256
MLA_KV_RANK = 128
MLA_NOPE = 64
MLA_ROPE = 32
MLA_V = 64

SWA_Q_HEADS = 16
SWA_KV_HEADS = 4
SWA_HEAD_DIM = 64
SWA_WINDOW = 128
SWA_BLOCK = 128

NA_HEADS = 16
NA_HEAD_DIM = 64
NA_ROWS = 8
NA_COLS = 16

LANES = 128
HEAD_V = 64
CHUNK_UNROLL = 4
V_ROWS = 80
VMEM_LIMIT = 56 * 1024 * 1024
LOG2E = math.log2(math.e)


def _cparams(n_axes):
    return pltpu.CompilerParams(dimension_semantics=("arbitrary",) * n_axes,
                                vmem_limit_bytes=VMEM_LIMIT)


def _const_spec(shape):
    nd = len(shape)
    return pl.BlockSpec(shape, lambda *_: (0,) * nd, pipeline_mode=pl.Buffered(1))


def _rms(x):
    return x * lax.rsqrt(jnp.mean(x * x, axis=-1, keepdims=True) + EPS)


def _norm_mod(x, g, shift, scale):
    return (_rms(x) * g) * (1.0 + scale) + shift


def _ada_kernel(c_ref, w_ref, b_ref, o_ref):
    c = c_ref[...]
    act = c / (1.0 + jnp.exp(-c))
    o_ref[0] = jnp.dot(act.astype(BF16), w_ref[0], preferred_element_type=F32) + b_ref[0]


def _ada_mods(c_rows, ada_w, ada_b):
    depth, d, n = ada_w.shape
    tn = 1536
    return pl.pallas_call(
        _ada_kernel,
        grid=(depth, n // tn),
        in_specs=[
            pl.BlockSpec((c_rows.shape[0], d), lambda i, j: (0, 0)),
            pl.BlockSpec((1, d, tn), lambda i, j: (i, 0, j)),
            pl.BlockSpec((1, 1, tn), lambda i, j: (i, 0, j)),
        ],
        out_specs=pl.BlockSpec((1, c_rows.shape[0], tn), lambda i, j: (i, 0, j)),
        out_shape=jax.ShapeDtypeStruct((depth, c_rows.shape[0], n), F32),
        compiler_params=_cparams(2),
        name="ada_mods",
    )(c_rows, ada_w.astype(BF16), ada_b.reshape(depth, 1, n))


class _Geom:
    def __init__(self, b, n, l):
        assert n % l == 0
        self.b, self.n, self.l = b, n, l
        self.tm = l
        self.tpb = (n + l) // l
        self.xtpb = n // l
        self.tiles = b * self.tpb
        self.rows = self.tiles * self.tm
        assert self.xtpb % 2 == 0
        self.qblocks = self.xtpb // 2 + 1

    def mod_row(self, t):
        return jnp.where(t % self.tpb == 0, self.b, t // self.tpb)

    def pos_block(self, t):
        return t % self.tpb

    def pair_block(self, t):
        j = t % self.tpb
        block = (t // self.tpb) * self.qblocks + jnp.where(j == 0, self.qblocks - 1, (j - 1) // 2)
        return block, jnp.where(j == 0, 0, (j - 1) % 2)


def _mod_spec(geom, layer, tile_of):
    return pl.BlockSpec((None, None, 1, N_MOD * D_MODEL),
                        lambda t: (layer, geom.mod_row(tile_of(t)), 0, 0))


def _row_spec(geom, width):
    return pl.BlockSpec((geom.tm, width), lambda t: (t, 0))


def _resid_specs(geom, resid):
    if not isinstance(resid, tuple):
        return [_row_spec(geom, D_MODEL)]

    def latent_map(t):
        return ((t // geom.tpb) * geom.xtpb + jnp.maximum(t % geom.tpb - 1, 0), 0)

    return [pl.BlockSpec((geom.tm, D_MODEL), latent_map),
            pl.BlockSpec((geom.tm, D_MODEL), lambda t: (t // geom.tpb, 0))]


def _resid_tile(refs, ctx_period):
    if ctx_period is None:
        return refs[0][...], refs[1:]
    is_ctx = pl.program_id(0) % ctx_period == 0
    return jnp.where(is_ctx, refs[1][...], refs[0][...]), refs[2:]


def _pos_spec(geom):
    return pl.BlockSpec((geom.tm, LANES), lambda t: (geom.pos_block(t), 0))


def _pos_spec_t(geom):
    return pl.BlockSpec((LANES, geom.tm), lambda t: (0, geom.pos_block(t)))


def _rope_cos_sin(n, rot_dim):
    t = jnp.arange(n, dtype=jnp.int32)
    row = (t // GRID_W).astype(F32)
    col = (t % GRID_W).astype(F32)
    n_freq = rot_dim // 4
    inv_freq = ROPE_BASE ** (-jnp.arange(n_freq, dtype=F32) / n_freq)
    ang = jnp.concatenate([row[:, None] * inv_freq, col[:, None] * inv_freq], axis=-1)
    return jnp.cos(ang), jnp.sin(ang)


def _rope_tables(geom, rot_dim, lead, scale):
    c, s = _rope_cos_sin(geom.n, rot_dim)
    half = rot_dim // 2
    pad = LANES - lead - rot_dim
    n, l = geom.n, geom.l
    cos_t = jnp.concatenate([jnp.ones((n, lead), F32), c, c, jnp.zeros((n, pad), F32)], axis=1)
    sin_t = jnp.concatenate([jnp.zeros((n, lead), F32), -s, s, jnp.zeros((n, pad), F32)], axis=1)
    cos_c = jnp.concatenate([jnp.ones((l, lead + 2 * half), F32), jnp.zeros((l, pad), F32)], axis=1)
    sin_c = jnp.zeros((l, LANES), F32)
    return (jnp.concatenate([cos_c, cos_t], axis=0) * scale,
            jnp.concatenate([sin_c, sin_t], axis=0) * scale)


def _ones_slot_vector(n_slots):
    row = np.zeros((n_slots, LANES), np.float32)
    row[:, HEAD_V] = 1.0
    return row.reshape(n_slots * LANES)


def _prep_mla_weights(w_in, w_qb, w_kvb):
    d = w_in.shape[0]
    pe0 = MLA_Q_RANK + MLA_KV_RANK
    ev = pe0 + 2 * np.arange(MLA_ROPE // 2)
    od = ev + 1
    z = jnp.zeros((d, LANES - MLA_ROPE), F32)
    win = jnp.concatenate([w_in[:, :pe0], w_in[:, ev], w_in[:, od], z, w_in[:, od], w_in[:, ev], z], axis=1)

    hd = MLA_NOPE + MLA_ROPE
    wq3 = w_qb.reshape(MLA_Q_RANK, MLA_HEADS, hd)
    nope, pe_e, pe_o = wq3[:, :, :MLA_NOPE], wq3[:, :, MLA_NOPE::2], wq3[:, :, MLA_NOPE + 1::2]
    zpad = jnp.zeros((MLA_Q_RANK, MLA_HEADS, LANES - hd), F32)
    zlead = jnp.zeros((MLA_Q_RANK, MLA_HEADS, MLA_NOPE), F32)
    wqa = jnp.concatenate([nope, pe_e, pe_o, zpad], axis=-1).reshape(MLA_Q_RANK, MLA_HEADS * LANES)
    wqb = jnp.concatenate([zlead, pe_o, pe_e, zpad], axis=-1).reshape(MLA_Q_RANK, MLA_HEADS * LANES)
    wq_t = jnp.concatenate([wqa, wqb], axis=1).T

    wkv3 = w_kvb.reshape(MLA_KV_RANK, MLA_HEADS, MLA_NOPE + MLA_V)
    zk = jnp.zeros((MLA_KV_RANK, MLA_HEADS, LANES - MLA_NOPE), F32)
    wk_top = jnp.concatenate([wkv3[:, :, :MLA_NOPE], zk], axis=-1).reshape(MLA_KV_RANK, MLA_HEADS * LANES)
    eye = np.zeros((LANES, MLA_HEADS, LANES), np.float32)
    for j in range(MLA_ROPE):
        eye[j, :, MLA_NOPE + j] = 1.0
    wk = jnp.concatenate([wk_top, jnp.asarray(eye.reshape(LANES, MLA_HEADS * LANES))], axis=0)
    zv = jnp.zeros((MLA_KV_RANK, MLA_HEADS, LANES - MLA_V), F32)
    wv_t = jnp.concatenate([wkv3[:, :, MLA_NOPE:], zv], axis=-1).reshape(MLA_KV_RANK, MLA_HEADS * LANES).T
    return win.astype(BF16), wq_t.astype(BF16), wk.astype(BF16), wv_t.astype(BF16)


def _mla_pre_kernel(*refs, ctx_period):
    x, refs = _resid_tile(refs, ctx_period)
    (mod_ref, g_ref, win_ref, qag_ref, kvag_ref, wqt_ref, wk_ref, wvt_ref,
     cqt_ref, sqt_ref, ck_ref, sk_ref, ones_ref, qt_out, k_out, vt_out) = refs
    nt = (((1,), (1,)), ((), ()))
    mod = mod_ref[...]
    h = _norm_mod(x, g_ref[...], mod[:, :D_MODEL], mod[:, D_MODEL:2 * D_MODEL]).astype(BF16)
    proj = jnp.dot(h, win_ref[...], preferred_element_type=F32)
    kv0 = MLA_Q_RANK
    pe0 = kv0 + MLA_KV_RANK
    qn = (_rms(proj[:, :kv0]) * qag_ref[...]).astype(BF16)
    kvn = _rms(proj[:, kv0:pe0]) * kvag_ref[...]
    qab_t = lax.dot_general(wqt_ref[...], qn, nt, preferred_element_type=F32)
    cqt, sqt = cqt_ref[...], sqt_ref[...]
    half = MLA_HEADS * LANES
    for hh in range(MLA_HEADS):
        lo = hh * LANES
        qt_out[0, lo:lo + LANES, :] = (qab_t[lo:lo + LANES] * cqt
                                       + qab_t[half + lo:half + lo + LANES] * sqt).astype(BF16)
    kpe = proj[:, pe0:pe0 + LANES] * ck_ref[...] + proj[:, pe0 + LANES:pe0 + 2 * LANES] * sk_ref[...]
    lat = jnp.concatenate([kvn, kpe], axis=1).astype(BF16)
    k_out[...] = jnp.dot(lat, wk_ref[...], preferred_element_type=F32).astype(BF16)
    vt = lax.dot_general(wvt_ref[...], kvn.astype(BF16), nt, preferred_element_type=F32)
    vt_out[0] = (vt + ones_ref[...]).astype(BF16)


def _mla_pre(geom, layer, resid, mods, g, w_in, qa_g, w_qb, kva_g, w_kvb):
    win, wq_t, wk, wv_t = _prep_mla_weights(w_in, w_qb, w_kvb)
    scale = (MLA_NOPE + MLA_ROPE) ** -0.5 * LOG2E
    cq, sq = _rope_tables(geom, MLA_ROPE, MLA_NOPE, scale)
    ck, sk = _rope_tables(geom, MLA_ROPE, 0, 1.0)
    width = MLA_HEADS * LANES
    tm = geom.tm
    t_spec = pl.BlockSpec((1, width, tm), lambda t: (t, 0, 0))
    t_shape = jax.ShapeDtypeStruct((geom.tiles, width, tm), BF16)
    q_spec = pl.BlockSpec((1, width, tm), lambda t: (geom.pair_block(t)[0], 0, geom.pair_block(t)[1]))
    q_shape = jax.ShapeDtypeStruct((geom.b * geom.qblocks, width, 2 * tm), BF16)
    split = isinstance(resid, tuple)
    return pl.pallas_call(
        functools.partial(_mla_pre_kernel, ctx_period=geom.tpb if split else None),
        grid=(geom.tiles,),
        in_specs=_resid_specs(geom, resid) + [
            _mod_spec(geom, layer, lambda t: t), _const_spec((1, D_MODEL)),
            _const_spec(win.shape), _const_spec((1, MLA_Q_RANK)), _const_spec((1, MLA_KV_RANK)),
            _const_spec(wq_t.shape), _const_spec(wk.shape), _const_spec(wv_t.shape),
            _pos_spec_t(geom), _pos_spec_t(geom), _pos_spec(geom), _pos_spec(geom),
            _const_spec((width, 1)),
        ],
        out_specs=[q_spec, _row_spec(geom, width), t_spec],
        out_shape=[q_shape, jax.ShapeDtypeStruct((geom.rows, width), BF16), t_shape],
        compiler_params=_cparams(1),
        name="mla_pre",
    )(*(resid if split else (resid,)), mods, g.reshape(1, D_MODEL), win, qa_g.reshape(1, -1),
      kva_g.reshape(1, -1), wq_t, wk, wv_t, cq.T, sq.T, ck, sk,
      jnp.asarray(_ones_slot_vector(MLA_HEADS).reshape(width, 1)))


def _mla_attn_kernel(qt_ref, k_ref, vt_ref, ot_ref, *s_bufs, cs, n_chunks, n_latent, with_ctx):
    tm = vt_ref.shape[2]
    tq = qt_ref.shape[2]
    t = pl.program_id(2)

    def normalised(accs):
        return jnp.concatenate([acc[:HEAD_V] * (1.0 / acc[HEAD_V:HEAD_V + 1]) for acc in accs], axis=0)

    def _context_tile():
        accs = []
        for j in range(2):
            s = jnp.dot(k_ref[0:tm, j * LANES:(j + 1) * LANES], qt_ref[0, j * LANES:(j + 1) * LANES, 0:tm],
                        preferred_element_type=F32)
            p = jnp.exp2(s - jnp.max(s, axis=0, keepdims=True)).astype(BF16)
            accs.append(jnp.dot(vt_ref[0, j * LANES:j * LANES + V_ROWS, :], p, preferred_element_type=F32))
        ot_ref[0, :, 0:tm] = normalised(accs).astype(BF16)
        ot_ref[0, :, tm:] = jnp.zeros((2 * HEAD_V, tq - tm), BF16)

    def _latent_tile(half):
        lanes = slice(half * tm, (half + 1) * tm)
        bufs = (s_bufs[2 * half], s_bufs[2 * half + 1])
        qts = [qt_ref[0, j * LANES:(j + 1) * LANES, lanes] for j in range(2)]

        def qk(c, j, u, s_ref):
            r0 = pl.multiple_of((c * cs + u) * tm, tm)
            s = jnp.dot(k_ref[pl.ds(r0, tm), j * LANES:(j + 1) * LANES], qts[j], preferred_element_type=F32)
            s_ref[j, u * tm:(u + 1) * tm, :] = s
            return jnp.max(s.reshape(tm // 8, 8, tm), axis=0)

        def pv(c, j, u, s_ref, m_new):
            p = jnp.exp2(s_ref[j, u * tm:(u + 1) * tm, :] - m_new).astype(BF16)
            return jnp.dot(vt_ref[c * cs + u, j * LANES:j * LANES + V_ROWS, :], p, preferred_element_type=F32)

        def first_scores(s_ref):
            maxes = []
            for j in range(2):
                part = qk(0, j, 0, s_ref)
                for u in range(1, cs):
                    part = jnp.maximum(part, qk(0, j, u, s_ref))
                maxes.append(jnp.max(part, axis=0, keepdims=True))
            return tuple(maxes)

        def step(c, s_cur, s_next, maxes, ms, accs, with_next):
            new_ms, new_accs, next_maxes = [], [], []
            for j in range(2):
                m_new = jnp.maximum(ms[j], maxes[j])
                alpha = jnp.exp2(ms[j] - m_new)
                part = qk(c + 1, j, 0, s_next) if with_next else None
                acc_new = None
                for u in range(cs):
                    if with_next and u + 1 < cs:
                        part = jnp.maximum(part, qk(c + 1, j, u + 1, s_next))
                    d = pv(c, j, u, s_cur, m_new)
                    acc_new = d if acc_new is None else acc_new + d
                new_accs.append(alpha * accs[j] + acc_new)
                new_ms.append(m_new)
                if with_next:
                    next_maxes.append(jnp.max(part, axis=0, keepdims=True))
            return tuple(next_maxes), tuple(new_ms), tuple(new_accs)

        ms = tuple(jnp.full((1, tm), NEG_INF, F32) for _ in range(2))
        accs = tuple(jnp.zeros((V_ROWS, tm), F32) for _ in range(2))
        maxes = first_scores(bufs[0])
        n_loop = (n_chunks - 1) // CHUNK_UNROLL

        def body(i, carry):
            maxes, ms, accs = carry
            for r in range(CHUNK_UNROLL):
                maxes, ms, accs = step(CHUNK_UNROLL * i + r, bufs[r % 2], bufs[(r + 1) % 2], maxes, ms, accs, True)
            return maxes, ms, accs

        maxes, ms, accs = lax.fori_loop(0, n_loop, body, (maxes, ms, accs))
        for c in range(n_loop * CHUNK_UNROLL, n_chunks):
            maxes, ms, accs = step(c, bufs[c % 2], bufs[(c + 1) % 2], maxes, ms, accs, c + 1 < n_chunks)
        ot_ref[0, :, lanes] = normalised(accs).astype(BF16)

    def _latent_tiles():
        for half in range(tq // tm):
            _latent_tile(half)

    if with_ctx:
        pl.when(t == n_latent)(_context_tile)
        pl.when(t < n_latent)(_latent_tiles)
    else:
        _latent_tiles()


def _mla_attn(geom, qt, k, vt, need_ctx):
    b, tpb, tm = geom.b, geom.tpb, geom.tm
    cs = 3 if tpb % 3 == 0 else (2 if tpb % 2 == 0 else 1)
    n_latent = geom.qblocks - 1
    tq = 2 * tm

    def q_map(bi, hp, t):
        return (bi * geom.qblocks + t, hp, 0)

    return pl.pallas_call(
        functools.partial(_mla_attn_kernel, cs=cs, n_chunks=tpb // cs, n_latent=n_latent, with_ctx=need_ctx),
        grid=(b, MLA_HEADS // 2, n_latent + (1 if need_ctx else 0)),
        in_specs=[pl.BlockSpec((1, 2 * LANES, tq), q_map),
                  pl.BlockSpec((tpb * tm, 2 * LANES), lambda bi, hp, t: (bi, hp)),
                  pl.BlockSpec((tpb, 2 * LANES, tm), lambda bi, hp, t: (bi, hp, 0))],
        out_specs=pl.BlockSpec((1, 2 * HEAD_V, tq), q_map),
        out_shape=jax.ShapeDtypeStruct((b * geom.qblocks, MLA_HEADS * HEAD_V, tq), BF16),
        scratch_shapes=[pltpu.VMEM((2, cs * tm, tm), F32)] * (2 * (tq // tm)),
        compiler_params=_cparams(3),
        name="mla_attn",
    )(qt, k, vt)


def _head_rope_tables(geom, scale):
    c, s = _rope_cos_sin(geom.n, SWA_HEAD_DIM)
    cos_t = jnp.concatenate([jnp.ones((geom.l, SWA_HEAD_DIM), F32), jnp.concatenate([c, c], axis=1)], axis=0)
    sin_t = jnp.concatenate([jnp.zeros((geom.l, SWA_HEAD_DIM), F32), jnp.concatenate([-s, s], axis=1)], axis=0)
    return cos_t * scale, sin_t * scale


def _prep_swa_weights(w_qkv):
    d = w_qkv.shape[0]
    dq = SWA_Q_HEADS * SWA_HEAD_DIM
    dkv = SWA_KV_HEADS * SWA_HEAD_DIM

    def rope_pair(w, heads):
        w3 = w.reshape(d, heads, SWA_HEAD_DIM)
        e, o = w3[:, :, 0::2], w3[:, :, 1::2]
        return (jnp.concatenate([e, o], axis=-1).reshape(d, heads * SWA_HEAD_DIM),
                jnp.concatenate([o, e], axis=-1).reshape(d, heads * SWA_HEAD_DIM))

    wqa, wqb = rope_pair(w_qkv[:, :dq], SWA_Q_HEADS)
    wka, wkb = rope_pair(w_qkv[:, dq:dq + dkv], SWA_KV_HEADS)
    w_t = jnp.concatenate([wqa, wqb, w_qkv[:, dq + dkv:]], axis=1).T
    return w_t.astype(BF16), jnp.concatenate([wka, wkb], axis=1).astype(BF16)


def _swa_pre_kernel(x_ref, mod_ref, g_ref, wt_ref, wk_ref, cqt_ref, sqt_ref, ck_ref, sk_ref,
                    qt_out, k_out, vt_out):
    nt = (((1,), (1,)), ((), ()))
    mod = mod_ref[...]
    h = _norm_mod(x_ref[...], g_ref[...], mod[:, :D_MODEL], mod[:, D_MODEL:2 * D_MODEL]).astype(BF16)
    tm = h.shape[0]
    dq = SWA_Q_HEADS * SWA_HEAD_DIM
    dkv = SWA_KV_HEADS * SWA_HEAD_DIM
    r = lax.dot_general(wt_ref[...], h, nt, preferred_element_type=F32)
    qa = r[:dq].reshape(SWA_Q_HEADS, SWA_HEAD_DIM, tm)
    qb = r[dq:2 * dq].reshape(SWA_Q_HEADS, SWA_HEAD_DIM, tm)
    qt_out[0] = (qa * cqt_ref[...][None] + qb * sqt_ref[...][None]).reshape(dq, tm).astype(BF16)
    vt_out[0] = r[2 * dq:].astype(BF16)
    kab = jnp.dot(h, wk_ref[...], preferred_element_type=F32)
    k_out[...] = (kab[:, :dkv] * ck_ref[...] + kab[:, dkv:] * sk_ref[...]).astype(BF16)


def _swa_pre(geom, layer, xall, mods, g, w_qkv):
    w_t, wk = _prep_swa_weights(w_qkv)
    cq, sq = _head_rope_tables(geom, SWA_HEAD_DIM ** -0.5 * LOG2E)
    ck, sk = _head_rope_tables(geom, 1.0)
    ck, sk = jnp.tile(ck, (1, SWA_KV_HEADS)), jnp.tile(sk, (1, SWA_KV_HEADS))
    tm = geom.tm
    dq, dkv = SWA_Q_HEADS * SWA_HEAD_DIM, SWA_KV_HEADS * SWA_HEAD_DIM
    qt_pos = pl.BlockSpec((SWA_HEAD_DIM, tm), lambda t: (0, geom.pos_block(t)))
    k_pos = pl.BlockSpec((tm, dkv), lambda t: (geom.pos_block(t), 0))
    return pl.pallas_call(
        _swa_pre_kernel,
        grid=(geom.tiles,),
        in_specs=[
            _row_spec(geom, D_MODEL), _mod_spec(geom, layer, lambda t: t), _const_spec((1, D_MODEL)),
            _const_spec(w_t.shape), _const_spec(wk.shape), qt_pos, qt_pos, k_pos, k_pos,
        ],
        out_specs=[pl.BlockSpec((1, dq, tm), lambda t: (t, 0, 0)), _row_spec(geom, dkv),
                   pl.BlockSpec((1, dkv, tm), lambda t: (t, 0, 0))],
        out_shape=[jax.ShapeDtypeStruct((geom.tiles, dq, tm), BF16),
                   jax.ShapeDtypeStruct((geom.rows, dkv), BF16),
                   jax.ShapeDtypeStruct((geom.tiles, dkv, tm), BF16)],
        compiler_params=_cparams(1),
        name="swa_pre",
    )(xall, mods, g.reshape(1, D_MODEL), w_t, wk, cq.T, sq.T, ck, sk)


def _na_pre_kernel(x_ref, mod_ref, g_ref, wt_ref, wk_ref, qt_out, k_out, vt_out):
    nt = (((1,), (1,)), ((), ()))
    mod = mod_ref[...]
    h = _norm_mod(x_ref[...], g_ref[...], mod[:, :D_MODEL], mod[:, D_MODEL:2 * D_MODEL]).astype(BF16)
    dm = NA_HEADS * NA_HEAD_DIM
    r = lax.dot_general(wt_ref[...], h, nt, preferred_element_type=F32)
    qt_out[0] = r[:dm].astype(BF16)
    vt_out[0] = r[dm:].astype(BF16)
    k_out[...] = jnp.dot(h, wk_ref[...], preferred_element_type=F32).astype(BF16)


def _na_pre(geom, layer, xall, mods, g, w_qkv):
    dm = NA_HEADS * NA_HEAD_DIM
    scale = NA_HEAD_DIM ** -0.5 * LOG2E
    w_t = jnp.concatenate([w_qkv[:, :dm] * scale, w_qkv[:, 2 * dm:]], axis=1).T.astype(BF16)
    wk = w_qkv[:, dm:2 * dm].astype(BF16)
    tm = geom.tm
    t_spec = pl.BlockSpec((1, dm, tm), lambda t: (t, 0, 0))
    t_shape = jax.ShapeDtypeStruct((geom.tiles, dm, tm), BF16)
    return pl.pallas_call(
        _na_pre_kernel,
        grid=(geom.tiles,),
        in_specs=[_row_spec(geom, D_MODEL), _mod_spec(geom, layer, lambda t: t), _const_spec((1, D_MODEL)),
                  _const_spec(w_t.shape), _const_spec(wk.shape)],
        out_specs=[t_spec, _row_spec(geom, dm), t_spec],
        out_shape=[t_shape, jax.ShapeDtypeStruct((geom.rows, dm), BF16), t_shape],
        compiler_params=_cparams(1),
        name="na_pre",
    )(xall, mods, g.reshape(1, D_MODEL), w_t, wk)


class _BandJob:
    def __init__(self, rhs, slot, vrow, bias, sink, write):
        self.rhs, self.slot, self.vrow, self.bias, self.sink, self.write = rhs, slot, vrow, bias, sink, write


def _slot_rhs(qt, upper):
    zeros = jnp.zeros_like(qt)
    return jnp.concatenate([zeros, qt] if upper else [qt, zeros], axis=0)


def _run_band_jobs(jobs, k_refs, vt_refs, s_bufs):
    sizes = [r.shape[0] for r in k_refs]
    offs = [sum(sizes[:i]) for i in range(len(sizes))]
    nblk = len(sizes)

    def qk(job, u, buf):
        nk, off = sizes[u], offs[u]
        s = jnp.dot(k_refs[u][:, job.slot * LANES:(job.slot + 1) * LANES], job.rhs,
                    preferred_element_type=F32) + job.bias(off, nk)
        buf[off:off + nk, :] = s
        return jnp.max(s.reshape(nk // 8, 8, s.shape[1]), axis=0)

    def pv(job, u, buf, m):
        nk, off = sizes[u], offs[u]
        p = jnp.exp2(buf[off:off + nk, :] - m).astype(BF16)
        lhs = jnp.concatenate([vt_refs[u][0, job.vrow:job.vrow + HEAD_V, :],
                               jnp.ones((V_ROWS - HEAD_V, nk), BF16)], axis=0)
        return jnp.dot(lhs, p, preferred_element_type=F32)

    part = qk(jobs[0], 0, s_bufs[0])
    for u in range(1, nblk):
        part = jnp.maximum(part, qk(jobs[0], u, s_bufs[0]))
    for i, job in enumerate(jobs):
        nxt = jobs[i + 1] if i + 1 < len(jobs) else None
        cur_buf, nxt_buf = s_bufs[i % 2], s_bufs[(i + 1) % 2]
        m = jnp.max(part, axis=0, keepdims=True)
        if job.sink is not None:
            m = jnp.maximum(m, job.sink)
        part = qk(nxt, 0, nxt_buf) if nxt is not None else None
        acc = None
        for u in range(nblk):
            if nxt is not None and u + 1 < nblk:
                part = jnp.maximum(part, qk(nxt, u + 1, nxt_buf))
            d = pv(job, u, cur_buf, m)
            acc = d if acc is None else acc + d
        denom = acc[HEAD_V:HEAD_V + 1]
        if job.sink is not None:
            denom = denom + jnp.exp2(job.sink - m)
        job.write(acc[:HEAD_V] * (1.0 / denom))


def _band_variant(t, tiles_x):
    return jnp.where(t >= tiles_x, 3, jnp.where(t == 0, 0, jnp.where(t == tiles_x - 1, 2, 1)))


def _swa_attn_kernel(qt_ref, kp, kc, kn, kx, vp, vc, vn, vx, mask_ref, sink_ref, ot_ref, s_a, s_b):
    tq = qt_ref.shape[2]
    grp = SWA_Q_HEADS // SWA_KV_HEADS
    lane = lax.broadcasted_iota(jnp.int32, (1, 2 * tq), 1)
    jobs = []
    for g in range(SWA_KV_HEADS):
        for i in range(grp // 2):
            h0 = g * grp + 2 * i
            qt = jnp.concatenate([qt_ref[0, h0 * HEAD_V:(h0 + 1) * HEAD_V, :],
                                  qt_ref[0, (h0 + 1) * HEAD_V:(h0 + 2) * HEAD_V, :]], axis=1)
            sink = jnp.where(lane < tq, sink_ref[h0], sink_ref[h0 + 1]) * LOG2E

            def write(o, h0=h0):
                ot_ref[0, h0 * HEAD_V:(h0 + 1) * HEAD_V, :] = o[:, :tq].astype(BF16)
                ot_ref[0, (h0 + 1) * HEAD_V:(h0 + 2) * HEAD_V, :] = o[:, tq:].astype(BF16)

            jobs.append(_BandJob(_slot_rhs(qt, g % 2 == 1), g // 2, g * HEAD_V,
                                 lambda off, nk: mask_ref[0, off:off + nk, :], sink, write))
    _run_band_jobs(jobs, [kp, kc, kn, kx], [vp, vc, vn, vx], (s_a, s_b))


def _swa_attn(geom, qt, k, vt, sink, need_ctx):
    b, n, l, tpb, tm = geom.b, geom.n, geom.l, geom.tpb, geom.tm
    tq = SWA_BLOCK
    per = tm // tq
    tiles_x = n // tq
    nt = tiles_x + (l // tq if need_ctx else 0)
    dq, dkv = SWA_Q_HEADS * SWA_HEAD_DIM, SWA_KV_HEADS * SWA_HEAD_DIM
    width = 3 * tq + l
    mask = jnp.asarray(np.tile(_swa_mask_t(l), (1, 1, 2)))

    def q_map(bi, t):
        return (bi * tpb + jnp.where(t < tiles_x, 1 + t // per, 0), 0,
                jnp.where(t < tiles_x, t % per, t - tiles_x))

    def band_tile(t, delta):
        return jnp.clip(jnp.minimum(t, tiles_x - 1) + delta, 0, tiles_x - 1)

    k_band = [pl.BlockSpec((tq, dkv), lambda bi, t, dl=dl: (bi * tpb * per + per + band_tile(t, dl), 0))
              for dl in (-1, 0, 1)]
    v_band = [pl.BlockSpec((1, dkv, tq), lambda bi, t, dl=dl: (bi * tpb + 1 + band_tile(t, dl) // per, 0,
                                                                band_tile(t, dl) % per))
              for dl in (-1, 0, 1)]
    in_specs = ([pl.BlockSpec((1, dq, tq), q_map)]
                + k_band + [pl.BlockSpec((l, dkv), lambda bi, t: (bi * tpb, 0))]
                + v_band + [pl.BlockSpec((1, dkv, l), lambda bi, t: (bi * tpb, 0, 0))]
                + [pl.BlockSpec((1, width, 2 * tq), lambda bi, t: (_band_variant(t, tiles_x), 0, 0)),
                   pl.BlockSpec(memory_space=pltpu.SMEM)])
    return pl.pallas_call(
        _swa_attn_kernel,
        grid=(b, nt),
        in_specs=in_specs,
        out_specs=pl.BlockSpec((1, dq, tq), q_map),
        out_shape=jax.ShapeDtypeStruct((geom.tiles, dq, tm), BF16),
        scratch_shapes=[pltpu.VMEM((width, 2 * tq), F32), pltpu.VMEM((width, 2 * tq), F32)],
        compiler_params=_cparams(2),
        name="swa_attn",
    )(qt, k, k, k, k, vt, vt, vt, vt, mask, sink)


def _swa_mask_t(l):
    tq = SWA_BLOCK
    j = np.arange(tq)[:, None]
    i = np.arange(tq)[None, :]
    prev_ok = (SWA_BLOCK + i - j) <= SWA_WINDOW
    next_ok = (SWA_BLOCK + j - i) <= SWA_WINDOW
    cur_ok = np.abs(i - j) <= SWA_WINDOW
    none = np.zeros((tq, tq), bool)
    ctx_ok = np.ones((l, tq), bool)
    variants = [
        np.concatenate([none, cur_ok, next_ok, ctx_ok], axis=0),
        np.concatenate([prev_ok, cur_ok, next_ok, ctx_ok], axis=0),
        np.concatenate([prev_ok, cur_ok, none, ctx_ok], axis=0),
        np.concatenate([none, none, none, ctx_ok], axis=0),
    ]
    return np.where(np.stack(variants), 0.0, NEG_INF).astype(np.float32)


NA_TILE_ROWS = 4
NA_HEADS_PER_STEP = 16


def _na_bias_t(rel_bias, l):
    col = np.arange(GRID_W)
    col_start = np.clip(col - NA_COLS // 2, 0, GRID_W - NA_COLS)
    col_valid = (col[None, :] >= col_start[:, None]) & (col[None, :] < col_start[:, None] + NA_COLS)
    dc_idx = np.clip(col[None, :] - col[:, None] + NA_COLS - 1, 0, 2 * NA_COLS - 2)
    bias_cols = (rel_bias.astype(F32) * LOG2E)[:, :, dc_idx]
    bias_cols = jnp.where(col_valid, bias_cols, NEG_INF)
    r = NA_TILE_ROWS
    a = np.arange(r)[:, None]
    c = np.arange(3 * r)[None, :]
    dr_idx = np.clip(c - r - a + NA_ROWS - 1, 0, 2 * NA_ROWS - 2)
    tiles = bias_cols[:, dr_idx]
    tiles = tiles.transpose(0, 2, 4, 1, 3)
    half = NA_ROWS // 2
    interior = (c - r >= a - half) & (c - r < a - half + NA_ROWS)
    first = (c >= r) & (c < r + NA_ROWS) & (a >= 0)
    last = (c >= 2 * r - NA_ROWS) & (c < 2 * r) & (a >= 0)
    none = np.zeros_like(interior)
    h = rel_bias.shape[0]
    out = []
    for ok in (first, interior, last, none):
        ok5 = ok.T[None, :, None, :, None]
        t = jnp.where(ok5, tiles, NEG_INF).reshape(h, 3 * r * GRID_W, r * GRID_W)
        out.append(jnp.concatenate([t, jnp.zeros((h, l, r * GRID_W), F32)], axis=1))
    return jnp.stack(out)


def _na_attn_kernel(qt_ref, kp, kc, kn, kx, vp, vc, vn, vx, bias_ref, ot_ref, s_a, s_b):
    jobs = []
    for hh in range(NA_HEADS_PER_STEP):
        rows = slice(hh * HEAD_V, (hh + 1) * HEAD_V)

        def write(o, rows=rows):
            ot_ref[0, rows, :] = o.astype(BF16)

        jobs.append(_BandJob(_slot_rhs(qt_ref[0, rows, :], hh % 2 == 1), hh // 2, hh * HEAD_V,
                             lambda off, nk, hh=hh: bias_ref[0, hh, off:off + nk, :], None, write))
    _run_band_jobs(jobs, [kp, kc, kn, kx], [vp, vc, vn, vx], (s_a, s_b))


def _na_attn(geom, qt, k, vt, bias, need_ctx):
    b, n, l, tpb, tm = geom.b, geom.n, geom.l, geom.tpb, geom.tm
    tiles_x = geom.xtpb
    nt = tiles_x + (1 if need_ctx else 0)
    hb = NA_HEADS_PER_STEP
    hw = hb * NA_HEAD_DIM
    width = 3 * tm + l

    def tile_of(bi, t):
        return bi * tpb + jnp.where(t < tiles_x, 1 + t, 0)

    def band_tile(bi, t, delta):
        return bi * tpb + 1 + jnp.clip(jnp.minimum(t, tiles_x - 1) + delta, 0, tiles_x - 1)

    q_spec = pl.BlockSpec((1, hw, tm), lambda g, t, bi: (tile_of(bi, t), g, 0))
    k_band = [pl.BlockSpec((tm, hw), lambda g, t, bi, dl=dl: (band_tile(bi, t, dl), g)) for dl in (-1, 0, 1)]
    v_band = [pl.BlockSpec((1, hw, tm), lambda g, t, bi, dl=dl: (band_tile(bi, t, dl), g, 0)) for dl in (-1, 0, 1)]
    in_specs = ([q_spec] + k_band + [pl.BlockSpec((l, hw), lambda g, t, bi: (bi * tpb, g))]
                + v_band + [pl.BlockSpec((1, hw, l), lambda g, t, bi: (bi * tpb, g, 0))]
                + [pl.BlockSpec((1, hb, width, tm), lambda g, t, bi: (_band_variant(t, tiles_x), g, 0, 0))])
    return pl.pallas_call(
        _na_attn_kernel,
        grid=(NA_HEADS // hb, nt, b),
        in_specs=in_specs,
        out_specs=q_spec,
        out_shape=jax.ShapeDtypeStruct((geom.tiles, NA_HEADS * NA_HEAD_DIM, tm), BF16),
        scratch_shapes=[pltpu.VMEM((width, tm), F32), pltpu.VMEM((width, tm), F32)],
        compiler_params=_cparams(3),
        name="na_attn",
    )(qt, k, k, k, k, vt, vt, vt, vt, bias)


def _post_kernel(*refs, final, ctx_period):
    x, refs = _resid_tile(refs, ctx_period)
    if final:
        o_ref, mod_ref, wo_ref, g_ref, w1_ref, w2_ref, gout_ref, out_ref = refs
    else:
        o_ref, mod_ref, wo_ref, g_ref, w1_ref, w2_ref, out_ref = refs
    d = D_MODEL
    mod = mod_ref[...]
    g1, sh2, sc2, g2 = mod[:, 2 * d:3 * d], mod[:, 3 * d:4 * d], mod[:, 4 * d:5 * d], mod[:, 5 * d:6 * d]
    y = lax.dot_general(o_ref[0], wo_ref[...], (((0,), (0,)), ((), ())), preferred_element_type=F32)
    x1 = x + g1 * y
    h2 = _norm_mod(x1, g_ref[...], sh2, sc2).astype(BF16)
    acc = jnp.zeros(x1.shape, F32)
    for c in range(D_FF // d):
        u = jnp.maximum(jnp.dot(h2, w1_ref[:, c * d:(c + 1) * d], preferred_element_type=F32), 0.0)
        acc = acc + jnp.dot((u * u).astype(BF16), w2_ref[c * d:(c + 1) * d, :], preferred_element_type=F32)
    x2 = x1 + g2 * acc
    if final:
        x2 = _rms(x2) * gout_ref[...]
    out_ref[...] = x2


def _post(geom, layer, resid, ot, ot_paired, mods, w_o, g, w1, w2, g_out):
    final = g_out is not None
    split = isinstance(resid, tuple)
    assert not (final and split)
    tm = geom.tm
    if final:
        tiles = geom.b * geom.xtpb
        tile_of = lambda t: (t // geom.xtpb) * geom.tpb + 1 + t % geom.xtpb
    else:
        tiles = geom.tiles
        tile_of = lambda t: t
    x_specs = _resid_specs(geom, resid) if split else [pl.BlockSpec((tm, D_MODEL), lambda t: (tile_of(t), 0))]
    if ot_paired:
        o_spec = pl.BlockSpec((1, D_MODEL, tm),
                              lambda t: (geom.pair_block(tile_of(t))[0], 0, geom.pair_block(tile_of(t))[1]))
    else:
        o_spec = pl.BlockSpec((1, D_MODEL, tm), lambda t: (tile_of(t), 0, 0))
    in_specs = x_specs + [o_spec, _mod_spec(geom, layer, tile_of),
                          _const_spec(w_o.shape), _const_spec((1, D_MODEL)), _const_spec(w1.shape),
                          _const_spec(w2.shape)]
    args = list(resid if split else (resid,)) + [ot, mods, w_o.astype(BF16), g.reshape(1, D_MODEL),
                                                  w1.astype(BF16), w2.astype(BF16)]
    if final:
        in_specs.append(_const_spec((1, D_MODEL)))
        args.append(g_out.reshape(1, D_MODEL))
    return pl.pallas_call(
        functools.partial(_post_kernel, final=final, ctx_period=geom.tpb if split else None),
        grid=(tiles,),
        in_specs=in_specs,
        out_specs=pl.BlockSpec((tm, D_MODEL), lambda t: (t, 0)),
        out_shape=jax.ShapeDtypeStruct((tiles * tm, D_MODEL), F32),
        compiler_params=_cparams(1),
        name="post_mlp",
    )(*args)


def kernel(x, c, ctx, c_ctx, ada_w, ada_b, norm_mix_g, norm_mlp_g, norm_out_g, mlp_w1, mlp_w2,
           mla_w_in, mla_qa_g, mla_w_qb, mla_kva_g, mla_w_kvb, mla_w_o,
           swa_w_qkv, swa_sink, swa_w_o, na_w_qkv, na_rel_bias, na_w_o):
    b, n, d = x.shape
    l = ctx.shape[1]
    assert d == D_MODEL and n % GRID_W == 0 and l % SWA_BLOCK == 0 and l == NA_TILE_ROWS * GRID_W
    geom = _Geom(b, n, l)

    mod_rows = -(-(b + 1) // 8) * 8
    c_rows = jnp.concatenate([c, c_ctx[None, :], jnp.zeros((mod_rows - b - 1, d), F32)], axis=0)
    mods = _ada_mods(c_rows, ada_w, ada_b).reshape(DEPTH, mod_rows, 1, N_MOD * d)

    xall = (x.reshape(b * n, d), ctx.reshape(b * l, d))
    for i in range(DEPTH):
        need_ctx = i < DEPTH - 1
        kind, j = i % 3, i // 3
        if kind == 0:
            qt, k, vt = _mla_pre(geom, i, xall, mods, norm_mix_g[i], mla_w_in[j], mla_qa_g[j], mla_w_qb[j],
                                 mla_kva_g[j], mla_w_kvb[j])
            o = _mla_attn(geom, qt, k, vt, need_ctx)
            w_o = mla_w_o[j]
        elif kind == 1:
            qt, k, vt = _swa_pre(geom, i, xall, mods, norm_mix_g[i], swa_w_qkv[j])
            o = _swa_attn(geom, qt, k, vt, swa_sink[j], need_ctx)
            w_o = swa_w_o[j]
        else:
            qt, k, vt = _na_pre(geom, i, xall, mods, norm_mix_g[i], na_w_qkv[j])
            o = _na_attn(geom, qt, k, vt, _na_bias_t(na_rel_bias[j], l), need_ctx)
            w_o = na_w_o[j]
        g_out = norm_out_g if i == DEPTH - 1 else None
        xall = _post(geom, i, xall, o, kind == 0, mods, w_o, norm_mlp_g[i], mlp_w1[i], mlp_w2[i], g_out)
    return xall.reshape(b, n, d)
```

```python
import functools
import math

import numpy as np
import jax
import jax.numpy as jnp
from jax import lax
from jax.experimental import pallas as pl
from jax.experimental.pallas import tpu as pltpu

F32 = jnp.float32
BF16 = jnp.bfloat16

D_MODEL = 1024
D_FF = 4 * D_MODEL
DEPTH = 4
N_MOD = 6
GRID_W = 64
EPS = 1e-6
ROPE_BASE = 10000.0
NEG_INF = -1e30

MLA_HEADS = 16
MLA_Q_RANK = 256
MLA_KV_RANK = 128
MLA_NOPE = 64
MLA_ROPE = 32
MLA_V = 64

SWA_Q_HEADS = 16
SWA_KV_HEADS = 4
SWA_HEAD_DIM = 64
SWA_WINDOW = 128
SWA_BLOCK = 128

NA_HEADS = 16
NA_HEAD_DIM = 64
NA_ROWS = 8
NA_COLS = 16

LANES = 128
HEAD_V = 64
CHUNK_UNROLL = 4
V_ROWS = 80
VMEM_LIMIT = 56 * 1024 * 1024
LOG2E = math.log2(math.e)


def _cparams(n_axes):
    return pltpu.CompilerParams(dimension_semantics=("arbitrary",) * n_axes,
                                vmem_limit_bytes=VMEM_LIMIT)


def _const_spec(shape):
    nd = len(shape)
    return pl.BlockSpec(shape, lambda *_: (0,) * nd, pipeline_mode=pl.Buffered(1))


def _rms(x):
    return x * lax.rsqrt(jnp.mean(x * x, axis=-1, keepdims=True) + EPS)


def _norm_mod(x, g, shift, scale):
    return (_rms(x) * g) * (1.0 + scale) + shift


def _ada_kernel(c_ref, w_ref, b_ref, o_ref):
    c = c_ref[...]
    act = c / (1.0 + jnp.exp(-c))
    o_ref[0] = jnp.dot(act.astype(BF16), w_ref[0], preferred_element_type=F32) + b_ref[0]


def _ada_mods(c_rows, ada_w, ada_b):
    depth, d, n = ada_w.shape
    tn = 1536
    return pl.pallas_call(
        _ada_kernel,
        grid=(depth, n // tn),
        in_specs=[
            pl.BlockSpec((c_rows.shape[0], d), lambda i, j: (0, 0)),
            pl.BlockSpec((1, d, tn), lambda i, j: (i, 0, j)),
            pl.BlockSpec((1, 1, tn), lambda i, j: (i, 0, j)),
        ],
        out_specs=pl.BlockSpec((1, c_rows.shape[0], tn), lambda i, j: (i, 0, j)),
        out_shape=jax.ShapeDtypeStruct((depth, c_rows.shape[0], n), F32),
        compiler_params=_cparams(2),
        name="ada_mods",
    )(c_rows, ada_w.astype(BF16), ada_b.reshape(depth, 1, n))


class _Geom:
    def __init__(self, b, n, l):
        assert n % l == 0
        self.b, self.n, self.l = b, n, l
        self.tm = l
        self.tpb = (n + l) // l
        self.xtpb = n // l
        self.tiles = b * self.tpb
        self.rows = self.tiles * self.tm
        self.qgroup = next(g for g in (4, 2, 1) if self.xtpb % g == 0)
        self.qblocks = self.xtpb // self.qgroup + 1

    def mod_row(self, t):
        return jnp.where(t % self.tpb == 0, self.b, t // self.tpb)

    def pos_block(self, t):
        return t % self.tpb

    def pair_block(self, t):
        j = t % self.tpb
        block = (t // self.tpb) * self.qblocks + jnp.where(j == 0, self.qblocks - 1, (j - 1) // self.qgroup)
        return block, jnp.where(j == 0, 0, (j - 1) % self.qgroup)


def _mod_spec(geom, layer, tile_of):
    return pl.BlockSpec((None, None, 1, N_MOD * D_MODEL),
                        lambda t: (layer, geom.mod_row(tile_of(t)), 0, 0))


def _row_spec(geom, width):
    return pl.BlockSpec((geom.tm, width), lambda t: (t, 0))


def _resid_specs(geom, resid):
    if not isinstance(resid, tuple):
        return [_row_spec(geom, D_MODEL)]

    def latent_map(t):
        return ((t // geom.tpb) * geom.xtpb + jnp.maximum(t % geom.tpb - 1, 0), 0)

    return [pl.BlockSpec((geom.tm, D_MODEL), latent_map),
            pl.BlockSpec((geom.tm, D_MODEL), lambda t: (t // geom.tpb, 0))]


def _resid_tile(refs, ctx_period):
    if ctx_period is None:
        return refs[0][...], refs[1:]
    is_ctx = pl.program_id(0) % ctx_period == 0
    return jnp.where(is_ctx, refs[1][...], refs[0][...]), refs[2:]


def _pos_spec(geom):
    return pl.BlockSpec((geom.tm, LANES), lambda t: (geom.pos_block(t), 0))


def _pos_spec_t(geom):
    return pl.BlockSpec((LANES, geom.tm), lambda t: (0, geom.pos_block(t)))


def _rope_cos_sin(n, rot_dim):
    t = jnp.arange(n, dtype=jnp.int32)
    row = (t // GRID_W).astype(F32)
    col = (t % GRID_W).astype(F32)
    n_freq = rot_dim // 4
    inv_freq = ROPE_BASE ** (-jnp.arange(n_freq, dtype=F32) / n_freq)
    ang = jnp.concatenate([row[:, None] * inv_freq, col[:, None] * inv_freq], axis=-1)
    return jnp.cos(ang), jnp.sin(ang)


def _rope_tables(geom, rot_dim, lead, scale):
    c, s = _rope_cos_sin(geom.n, rot_dim)
    half = rot_dim // 2
    pad = LANES - lead - rot_dim
    n, l = geom.n, geom.l
    cos_t = jnp.concatenate([jnp.ones((n, lead), F32), c, c, jnp.zeros((n, pad), F32)], axis=1)
    sin_t = jnp.concatenate([jnp.zeros((n, lead), F32), -s, s, jnp.zeros((n, pad), F32)], axis=1)
    cos_c = jnp.concatenate([jnp.ones((l, lead + 2 * half), F32), jnp.zeros((l, pad), F32)], axis=1)
    sin_c = jnp.zeros((l, LANES), F32)
    return (jnp.concatenate([cos_c, cos_t], axis=0) * scale,
            jnp.concatenate([sin_c, sin_t], axis=0) * scale)


def _ones_slot_vector(n_slots):
    row = np.zeros((n_slots, LANES), np.float32)
    row[:, HEAD_V] = 1.0
    return row.reshape(n_slots * LANES)


def _prep_mla_weights(w_in, w_qb, w_kvb):
    d = w_in.shape[0]
    pe0 = MLA_Q_RANK + MLA_KV_RANK
    ev = pe0 + 2 * np.arange(MLA_ROPE // 2)
    od = ev + 1
    z = jnp.zeros((d, LANES - MLA_ROPE), F32)
    win = jnp.concatenate([w_in[:, :pe0], w_in[:, ev], w_in[:, od], z, w_in[:, od], w_in[:, ev], z], axis=1)

    hd = MLA_NOPE + MLA_ROPE
    wq3 = w_qb.reshape(MLA_Q_RANK, MLA_HEADS, hd)
    nope, pe_e, pe_o = wq3[:, :, :MLA_NOPE], wq3[:, :, MLA_NOPE::2], wq3[:, :, MLA_NOPE + 1::2]
    zpad = jnp.zeros((MLA_Q_RANK, MLA_HEADS, LANES - hd), F32)
    zlead = jnp.zeros((MLA_Q_RANK, MLA_HEADS, MLA_NOPE), F32)
    wqa = jnp.concatenate([nope, pe_e, pe_o, zpad], axis=-1).reshape(MLA_Q_RANK, MLA_HEADS * LANES)
    wqb = jnp.concatenate([zlead, pe_o, pe_e, zpad], axis=-1).reshape(MLA_Q_RANK, MLA_HEADS * LANES)
    wq_t = jnp.concatenate([wqa, wqb], axis=1).T

    wkv3 = w_kvb.reshape(MLA_KV_RANK, MLA_HEADS, MLA_NOPE + MLA_V)
    zk = jnp.zeros((MLA_KV_RANK, MLA_HEADS, LANES - MLA_NOPE), F32)
    wk_top = jnp.concatenate([wkv3[:, :, :MLA_NOPE], zk], axis=-1).reshape(MLA_KV_RANK, MLA_HEADS * LANES)
    eye = np.zeros((LANES, MLA_HEADS, LANES), np.float32)
    for j in range(MLA_ROPE):
        eye[j, :, MLA_NOPE + j] = 1.0
    wk = jnp.concatenate([wk_top, jnp.asarray(eye.reshape(LANES, MLA_HEADS * LANES))], axis=0)
    zv = jnp.zeros((MLA_KV_RANK, MLA_HEADS, LANES - MLA_V), F32)
    wv_t = jnp.concatenate([wkv3[:, :, MLA_NOPE:], zv], axis=-1).reshape(MLA_KV_RANK, MLA_HEADS * LANES).T
    return win.astype(BF16), wq_t.astype(BF16), wk.astype(BF16), wv_t.astype(BF16)


def _mla_pre_kernel(*refs, ctx_period):
    x, refs = _resid_tile(refs, ctx_period)
    (mod_ref, g_ref, win_ref, qag_ref, kvag_ref, wqt_ref, wk_ref, wvt_ref,
     cqt_ref, sqt_ref, ck_ref, sk_ref, ones_ref, qt_out, k_out, vt_out) = refs
    nt = (((1,), (1,)), ((), ()))
    mod = mod_ref[...]
    h = _norm_mod(x, g_ref[...], mod[:, :D_MODEL], mod[:, D_MODEL:2 * D_MODEL]).astype(BF16)
    proj = jnp.dot(h, win_ref[...], preferred_element_type=F32)
    kv0 = MLA_Q_RANK
    pe0 = kv0 + MLA_KV_RANK
    qn = (_rms(proj[:, :kv0]) * qag_ref[...]).astype(BF16)
    kvn = _rms(proj[:, kv0:pe0]) * kvag_ref[...]
    qab_t = lax.dot_general(wqt_ref[...], qn, nt, preferred_element_type=F32)
    cqt, sqt = cqt_ref[...], sqt_ref[...]
    half = MLA_HEADS * LANES
    for hh in range(MLA_HEADS):
        lo = hh * LANES
        qt_out[0, lo:lo + LANES, :] = (qab_t[lo:lo + LANES] * cqt
                                       + qab_t[half + lo:half + lo + LANES] * sqt).astype(BF16)
    kpe = proj[:, pe0:pe0 + LANES] * ck_ref[...] + proj[:, pe0 + LANES:pe0 + 2 * LANES] * sk_ref[...]
    lat = jnp.concatenate([kvn, kpe], axis=1).astype(BF16)
    k_out[...] = jnp.dot(lat, wk_ref[...], preferred_element_type=F32).astype(BF16)
    vt = lax.dot_general(wvt_ref[...], kvn.astype(BF16), nt, preferred_element_type=F32)
    vt_out[0] = (vt + ones_ref[...]).astype(BF16)


def _mla_pre(geom, layer, resid, mods, g, w_in, qa_g, w_qb, kva_g, w_kvb):
    win, wq_t, wk, wv_t = _prep_mla_weights(w_in, w_qb, w_kvb)
    scale = (MLA_NOPE + MLA_ROPE) ** -0.5 * LOG2E
    cq, sq = _rope_tables(geom, MLA_ROPE, MLA_NOPE, scale)
    ck, sk = _rope_tables(geom, MLA_ROPE, 0, 1.0)
    width = MLA_HEADS * LANES
    tm = geom.tm
    t_spec = pl.BlockSpec((1, width, tm), lambda t: (t, 0, 0))
    t_shape = jax.ShapeDtypeStruct((geom.tiles, width, tm), BF16)
    q_spec = pl.BlockSpec((1, width, tm), lambda t: (geom.pair_block(t)[0], 0, geom.pair_block(t)[1]))
    q_shape = jax.ShapeDtypeStruct((geom.b * geom.qblocks, width, geom.qgroup * tm), BF16)
    split = isinstance(resid, tuple)
    return pl.pallas_call(
        functools.partial(_mla_pre_kernel, ctx_period=geom.tpb if split else None),
        grid=(geom.tiles,),
        in_specs=_resid_specs(geom, resid) + [
            _mod_spec(geom, layer, lambda t: t), _const_spec((1, D_MODEL)),
            _const_spec(win.shape), _const_spec((1, MLA_Q_RANK)), _const_spec((1, MLA_KV_RANK)),
            _const_spec(wq_t.shape), _const_spec(wk.shape), _const_spec(wv_t.shape),
            _pos_spec_t(geom), _pos_spec_t(geom), _pos_spec(geom), _pos_spec(geom),
            _const_spec((width, 1)),
        ],
        out_specs=[q_spec, _row_spec(geom, width), t_spec],
        out_shape=[q_shape, jax.ShapeDtypeStruct((geom.rows, width), BF16), t_shape],
        compiler_params=_cparams(1),
        name="mla_pre",
    )(*(resid if split else (resid,)), mods, g.reshape(1, D_MODEL), win, qa_g.reshape(1, -1),
      kva_g.reshape(1, -1), wq_t, wk, wv_t, cq.T, sq.T, ck, sk,
      jnp.asarray(_ones_slot_vector(MLA_HEADS).reshape(width, 1)))


def _mla_attn_kernel(qt_ref, k_ref, vt_ref, ot_ref, *s_bufs, cs, n_chunks, n_latent, with_ctx):
    tm = vt_ref.shape[2]
    tq = qt_ref.shape[2]
    t = pl.program_id(2)

    def normalised(accs):
        return jnp.concatenate([acc[:HEAD_V] * (1.0 / acc[HEAD_V:HEAD_V + 1]) for acc in accs], axis=0)

    def _context_tile():
        accs = []
        for j in range(2):
            s = jnp.dot(k_ref[0:tm, j * LANES:(j + 1) * LANES], qt_ref[0, j * LANES:(j + 1) * LANES, 0:tm],
                        preferred_element_type=F32)
            p = jnp.exp2(s - jnp.max(s, axis=0, keepdims=True)).astype(BF16)
            accs.append(jnp.dot(vt_ref[0, j * LANES:j * LANES + V_ROWS, :], p, preferred_element_type=F32))
        ot_ref[0, :, 0:tm] = normalised(accs).astype(BF16)
        ot_ref[0, :, tm:] = jnp.zeros((2 * HEAD_V, tq - tm), BF16)

    n_tiles = tq // tm

    def qk(tile, c, par, j, u):
        r0 = pl.multiple_of((c * cs + u) * tm, tm)
        qt = qt_ref[0, j * LANES:(j + 1) * LANES, tile * tm:(tile + 1) * tm]
        s = jnp.dot(k_ref[pl.ds(r0, tm), j * LANES:(j + 1) * LANES], qt, preferred_element_type=F32)
        s_bufs[2 * tile + par][j, u * tm:(u + 1) * tm, :] = s
        return jnp.max(s.reshape(tm // 8, 8, tm), axis=0)

    def pv(tile, c, par, j, u, m_new):
        p = jnp.exp2(s_bufs[2 * tile + par][j, u * tm:(u + 1) * tm, :] - m_new).astype(BF16)
        return jnp.dot(vt_ref[c * cs + u, j * LANES:j * LANES + V_ROWS, :], p, preferred_element_type=F32)

    def first_scores():
        maxes = []
        for j in range(2):
            part = qk(0, 0, 0, j, 0)
            for u in range(1, cs):
                part = jnp.maximum(part, qk(0, 0, 0, j, u))
            maxes.append(jnp.max(part, axis=0, keepdims=True))
        return tuple(maxes)

    def step(tile, c, par, nxt, maxes, ms, accs):
        new_ms, new_accs, next_maxes = [], [], []
        for j in range(2):
            m_new = jnp.maximum(ms[j], maxes[j])
            alpha = jnp.exp2(ms[j] - m_new)
            part = qk(*nxt, j, 0) if nxt is not None else None
            acc_new = None
            for u in range(cs):
                if nxt is not None and u + 1 < cs:
                    part = jnp.maximum(part, qk(*nxt, j, u + 1))
                d = pv(tile, c, par, j, u, m_new)
                acc_new = d if acc_new is None else acc_new + d
            new_accs.append(alpha * accs[j] + acc_new)
            new_ms.append(m_new)
            if nxt is not None:
                next_maxes.append(jnp.max(part, axis=0, keepdims=True))
        return tuple(next_maxes), tuple(new_ms), tuple(new_accs)

    def _latent_tiles():
        n_loop = (n_chunks - 1) // CHUNK_UNROLL
        maxes = first_scores()
        for tile in range(n_tiles):
            ms = tuple(jnp.full((1, tm), NEG_INF, F32) for _ in range(2))
            accs = tuple(jnp.zeros((V_ROWS, tm), F32) for _ in range(2))

            def body(i, carry, tile=tile):
                maxes, ms, accs = carry
                for r in range(CHUNK_UNROLL):
                    c = CHUNK_UNROLL * i + r
                    maxes, ms, accs = step(tile, c, r % 2, (tile, c + 1, (r + 1) % 2), maxes, ms, accs)
                return maxes, ms, accs

            maxes, ms, accs = lax.fori_loop(0, n_loop, body, (maxes, ms, accs))
            for c in range(n_loop * CHUNK_UNROLL, n_chunks):
                if c + 1 < n_chunks:
                    nxt = (tile, c + 1, (c + 1) % 2)
                elif tile + 1 < n_tiles:
                    nxt = (tile + 1, 0, 0)
                else:
                    nxt = None
                maxes, ms, accs = step(tile, c, c % 2, nxt, maxes, ms, accs)
            ot_ref[0, :, tile * tm:(tile + 1) * tm] = normalised(accs).astype(BF16)

    if with_ctx:
        pl.when(t == n_latent)(_context_tile)
        pl.when(t < n_latent)(_latent_tiles)
    else:
        _latent_tiles()


def _mla_attn(geom, qt, k, vt, need_ctx):
    b, tpb, tm = geom.b, geom.tpb, geom.tm
    cs = 3 if tpb % 3 == 0 else (2 if tpb % 2 == 0 else 1)
    n_latent = geom.qblocks - 1
    tq = geom.qgroup * tm

    def q_map(bi, hp, t):
        return (bi * geom.qblocks + t, hp, 0)

    return pl.pallas_call(
        functools.partial(_mla_attn_kernel, cs=cs, n_chunks=tpb // cs, n_latent=n_latent, with_ctx=need_ctx),
        grid=(b, MLA_HEADS // 2, n_latent + (1 if need_ctx else 0)),
        in_specs=[pl.BlockSpec((1, 2 * LANES, tq), q_map),
                  pl.BlockSpec((tpb * tm, 2 * LANES), lambda bi, hp, t: (bi, hp)),
                  pl.BlockSpec((tpb, 2 * LANES, tm), lambda bi, hp, t: (bi, hp, 0))],
        out_specs=pl.BlockSpec((1, 2 * HEAD_V, tq), q_map),
        out_shape=jax.ShapeDtypeStruct((b * geom.qblocks, MLA_HEADS * HEAD_V, tq), BF16),
        scratch_shapes=[pltpu.VMEM((2, cs * tm, tm), F32)] * (2 * (tq // tm)),
        compiler_params=_cparams(3),
        name="mla_attn",
    )(qt, k, vt)


def _head_rope_tables(geom, scale):
    c, s = _rope_cos_sin(geom.n, SWA_HEAD_DIM)
    cos_t = jnp.concatenate([jnp.ones((geom.l, SWA_HEAD_DIM), F32), jnp.concatenate([c, c], axis=1)], axis=0)
    sin_t = jnp.concatenate([jnp.zeros((geom.l, SWA_HEAD_DIM), F32), jnp.concatenate([-s, s], axis=1)], axis=0)
    return cos_t * scale, sin_t * scale


def _prep_swa_weights(w_qkv):
    d = w_qkv.shape[0]
    dq = SWA_Q_HEADS * SWA_HEAD_DIM
    dkv = SWA_KV_HEADS * SWA_HEAD_DIM

    def rope_pair(w, heads):
        w3 = w.reshape(d, heads, SWA_HEAD_DIM)
        e, o = w3[:, :, 0::2], w3[:, :, 1::2]
        return (jnp.concatenate([e, o], axis=-1).reshape(d, heads * SWA_HEAD_DIM),
                jnp.concatenate([o, e], axis=-1).reshape(d, heads * SWA_HEAD_DIM))

    wqa, wqb = rope_pair(w_qkv[:, :dq], SWA_Q_HEADS)
    wka, wkb = rope_pair(w_qkv[:, dq:dq + dkv], SWA_KV_HEADS)
    w_t = jnp.concatenate([wqa, wqb, w_qkv[:, dq + dkv:]], axis=1).T
    return w_t.astype(BF16), jnp.concatenate([wka, wkb], axis=1).astype(BF16)


def _swa_pre_kernel(x_ref, mod_ref, g_ref, wt_ref, wk_ref, cqt_ref, sqt_ref, ck_ref, sk_ref,
                    qt_out, k_out, vt_out):
    nt = (((1,), (1,)), ((), ()))
    mod = mod_ref[...]
    h = _norm_mod(x_ref[...], g_ref[...], mod[:, :D_MODEL], mod[:, D_MODEL:2 * D_MODEL]).astype(BF16)
    tm = h.shape[0]
    dq = SWA_Q_HEADS * SWA_HEAD_DIM
    dkv = SWA_KV_HEADS * SWA_HEAD_DIM
    r = lax.dot_general(wt_ref[...], h, nt, preferred_element_type=F32)
    qa = r[:dq].reshape(SWA_Q_HEADS, SWA_HEAD_DIM, tm)
    qb = r[dq:2 * dq].reshape(SWA_Q_HEADS, SWA_HEAD_DIM, tm)
    qt_out[0] = (qa * cqt_ref[...][None] + qb * sqt_ref[...][None]).reshape(dq, tm).astype(BF16)
    vt_out[0] = r[2 * dq:].astype(BF16)
    kab = jnp.dot(h, wk_ref[...], preferred_element_type=F32)
    k_out[...] = (kab[:, :dkv] * ck_ref[...] + kab[:, dkv:] * sk_ref[...]).astype(BF16)


def _swa_pre(geom, layer, xall, mods, g, w_qkv):
    w_t, wk = _prep_swa_weights(w_qkv)
    cq, sq = _head_rope_tables(geom, SWA_HEAD_DIM ** -0.5 * LOG2E)
    ck, sk = _head_rope_tables(geom, 1.0)
    ck, sk = jnp.tile(ck, (1, SWA_KV_HEADS)), jnp.tile(sk, (1, SWA_KV_HEADS))
    tm = geom.tm
    dq, dkv = SWA_Q_HEADS * SWA_HEAD_DIM, SWA_KV_HEADS * SWA_HEAD_DIM
    qt_pos = pl.BlockSpec((SWA_HEAD_DIM, tm), lambda t: (0, geom.pos_block(t)))
    k_pos = pl.BlockSpec((tm, dkv), lambda t: (geom.pos_block(t), 0))
    return pl.pallas_call(
        _swa_pre_kernel,
        grid=(geom.tiles,),
        in_specs=[
            _row_spec(geom, D_MODEL), _mod_spec(geom, layer, lambda t: t), _const_spec((1, D_MODEL)),
            _const_spec(w_t.shape), _const_spec(wk.shape), qt_pos, qt_pos, k_pos, k_pos,
        ],
        out_specs=[pl.BlockSpec((1, dq, tm), lambda t: (t, 0, 0)), _row_spec(geom, dkv),
                   pl.BlockSpec((1, dkv, tm), lambda t: (t, 0, 0))],
        out_shape=[jax.ShapeDtypeStruct((geom.tiles, dq, tm), BF16),
                   jax.ShapeDtypeStruct((geom.rows, dkv), BF16),
                   jax.ShapeDtypeStruct((geom.tiles, dkv, tm), BF16)],
        compiler_params=_cparams(1),
        name="swa_pre",
    )(xall, mods, g.reshape(1, D_MODEL), w_t, wk, cq.T, sq.T, ck, sk)


def _na_pre_kernel(x_ref, mod_ref, g_ref, wt_ref, wk_ref, qt_out, k_out, vt_out):
    nt = (((1,), (1,)), ((), ()))
    mod = mod_ref[...]
    h = _norm_mod(x_ref[...], g_ref[...], mod[:, :D_MODEL], mod[:, D_MODEL:2 * D_MODEL]).astype(BF16)
    dm = NA_HEADS * NA_HEAD_DIM
    r = lax.dot_general(wt_ref[...], h, nt, preferred_element_type=F32)
    qt_out[0] = r[:dm].astype(BF16)
    vt_out[0] = r[dm:].astype(BF16)
    k_out[...] = jnp.dot(h, wk_ref[...], preferred_element_type=F32).astype(BF16)


def _na_pre(geom, layer, xall, mods, g, w_qkv):
    dm = NA_HEADS * NA_HEAD_DIM
    scale = NA_HEAD_DIM ** -0.5 * LOG2E
    w_t = jnp.concatenate([w_qkv[:, :dm] * scale, w_qkv[:, 2 * dm:]], axis=1).T.astype(BF16)
    wk = w_qkv[:, dm:2 * dm].astype(BF16)
    tm = geom.tm
    t_spec = pl.BlockSpec((1, dm, tm), lambda t: (t, 0, 0))
    t_shape = jax.ShapeDtypeStruct((geom.tiles, dm, tm), BF16)
    return pl.pallas_call(
        _na_pre_kernel,
        grid=(geom.tiles,),
        in_specs=[_row_spec(geom, D_MODEL), _mod_spec(geom, layer, lambda t: t), _const_spec((1, D_MODEL)),
                  _const_spec(w_t.shape), _const_spec(wk.shape)],
        out_specs=[t_spec, _row_spec(geom, dm), t_spec],
        out_shape=[t_shape, jax.ShapeDtypeStruct((geom.rows, dm), BF16), t_shape],
        compiler_params=_cparams(1),
        name="na_pre",
    )(xall, mods, g.reshape(1, D_MODEL), w_t, wk)


class _BandJob:
    def __init__(self, rhs, slot, vrow, bias, sink, write):
        self.rhs, self.slot, self.vrow, self.bias, self.sink, self.write = rhs, slot, vrow, bias, sink, write


def _slot_rhs(qt, upper):
    zeros = jnp.zeros_like(qt)
    return jnp.concatenate([zeros, qt] if upper else [qt, zeros], axis=0)


def _run_band_jobs(jobs, k_refs, vt_refs, s_bufs):
    sizes = [r.shape[0] for r in k_refs]
    offs = [sum(sizes[:i]) for i in range(len(sizes))]
    nblk = len(sizes)

    def qk(job, u, buf):
        nk, off = sizes[u], offs[u]
        s = jnp.dot(k_refs[u][:, job.slot * LANES:(job.slot + 1) * LANES], job.rhs,
                    preferred_element_type=F32) + job.bias(off, nk)
        buf[off:off + nk, :] = s
        return jnp.max(s.reshape(nk // 8, 8, s.shape[1]), axis=0)

    def pv(job, u, buf, m):
        nk, off = sizes[u], offs[u]
        p = jnp.exp2(buf[off:off + nk, :] - m).astype(BF16)
        lhs = jnp.concatenate([vt_refs[u][0, job.vrow:job.vrow + HEAD_V, :],
                               jnp.ones((V_ROWS - HEAD_V, nk), BF16)], axis=0)
        return jnp.dot(lhs, p, preferred_element_type=F32)

    part = qk(jobs[0], 0, s_bufs[0])
    for u in range(1, nblk):
        part = jnp.maximum(part, qk(jobs[0], u, s_bufs[0]))
    for i, job in enumerate(jobs):
        nxt = jobs[i + 1] if i + 1 < len(jobs) else None
        cur_buf, nxt_buf = s_bufs[i % 2], s_bufs[(i + 1) % 2]
        m = jnp.max(part, axis=0, keepdims=True)
        if job.sink is not None:
            m = jnp.maximum(m, job.sink)
        part = qk(nxt, 0, nxt_buf) if nxt is not None else None
        acc = None
        for u in range(nblk):
            if nxt is not None and u + 1 < nblk:
                part = jnp.maximum(part, qk(nxt, u + 1, nxt_buf))
            d = pv(job, u, cur_buf, m)
            acc = d if acc is None else acc + d
        denom = acc[HEAD_V:HEAD_V + 1]
        if job.sink is not None:
            denom = denom + jnp.exp2(job.sink - m)
        job.write(acc[:HEAD_V] * (1.0 / denom))


def _band_variant(t, tiles_x):
    return jnp.where(t >= tiles_x, 3, jnp.where(t == 0, 0, jnp.where(t == tiles_x - 1, 2, 1)))


def _swa_attn_kernel(qt_ref, kp, kc, kn, kx, vp, vc, vn, vx, mask_ref, sink_ref, ot_ref, s_a, s_b):
    tq = qt_ref.shape[2]
    grp = SWA_Q_HEADS // SWA_KV_HEADS
    lane = lax.broadcasted_iota(jnp.int32, (1, 2 * tq), 1)
    jobs = []
    for g in range(SWA_KV_HEADS):
        for i in range(grp // 2):
            h0 = g * grp + 2 * i
            qt = jnp.concatenate([qt_ref[0, h0 * HEAD_V:(h0 + 1) * HEAD_V, :],
                                  qt_ref[0, (h0 + 1) * HEAD_V:(h0 + 2) * HEAD_V, :]], axis=1)
            sink = jnp.where(lane < tq, sink_ref[h0], sink_ref[h0 + 1]) * LOG2E

            def write(o, h0=h0):
                ot_ref[0, h0 * HEAD_V:(h0 + 1) * HEAD_V, :] = o[:, :tq].astype(BF16)
                ot_ref[0, (h0 + 1) * HEAD_V:(h0 + 2) * HEAD_V, :] = o[:, tq:].astype(BF16)

            jobs.append(_BandJob(_slot_rhs(qt, g % 2 == 1), g // 2, g * HEAD_V,
                                 lambda off, nk: mask_ref[0, off:off + nk, :], sink, write))
    _run_band_jobs(jobs, [kp, kc, kn, kx], [vp, vc, vn, vx], (s_a, s_b))


def _swa_attn(geom, qt, k, vt, sink, need_ctx):
    b, n, l, tpb, tm = geom.b, geom.n, geom.l, geom.tpb, geom.tm
    tq = SWA_BLOCK
    per = tm // tq
    tiles_x = n // tq
    nt = tiles_x + (l // tq if need_ctx else 0)
    dq, dkv = SWA_Q_HEADS * SWA_HEAD_DIM, SWA_KV_HEADS * SWA_HEAD_DIM
    width = 3 * tq + l
    mask = jnp.asarray(np.tile(_swa_mask_t(l), (1, 1, 2)))

    def q_map(bi, t):
        return (bi * tpb + jnp.where(t < tiles_x, 1 + t // per, 0), 0,
                jnp.where(t < tiles_x, t % per, t - tiles_x))

    def band_tile(t, delta):
        return jnp.clip(jnp.minimum(t, tiles_x - 1) + delta, 0, tiles_x - 1)

    k_band = [pl.BlockSpec((tq, dkv), lambda bi, t, dl=dl: (bi * tpb * per + per + band_tile(t, dl), 0))
              for dl in (-1, 0, 1)]
    v_band = [pl.BlockSpec((1, dkv, tq), lambda bi, t, dl=dl: (bi * tpb + 1 + band_tile(t, dl) // per, 0,
                                                                band_tile(t, dl) % per))
              for dl in (-1, 0, 1)]
    in_specs = ([pl.BlockSpec((1, dq, tq), q_map)]
                + k_band + [pl.BlockSpec((l, dkv), lambda bi, t: (bi * tpb, 0))]
                + v_band + [pl.BlockSpec((1, dkv, l), lambda bi, t: (bi * tpb, 0, 0))]
                + [pl.BlockSpec((1, width, 2 * tq), lambda bi, t: (_band_variant(t, tiles_x), 0, 0)),
                   pl.BlockSpec(memory_space=pltpu.SMEM)])
    return pl.pallas_call(
        _swa_attn_kernel,
        grid=(b, nt),
        in_specs=in_specs,
        out_specs=pl.BlockSpec((1, dq, tq), q_map),
        out_shape=jax.ShapeDtypeStruct((geom.tiles, dq, tm), BF16),
        scratch_shapes=[pltpu.VMEM((width, 2 * tq), F32), pltpu.VMEM((width, 2 * tq), F32)],
        compiler_params=_cparams(2),
        name="swa_attn",
    )(qt, k, k, k, k, vt, vt, vt, vt, mask, sink)


def _swa_mask_t(l):
    tq = SWA_BLOCK
    j = np.arange(tq)[:, None]
    i = np.arange(tq)[None, :]
    prev_ok = (SWA_BLOCK + i - j) <= SWA_WINDOW
    next_ok = (SWA_BLOCK + j - i) <= SWA_WINDOW
    cur_ok = np.abs(i - j) <= SWA_WINDOW
    none = np.zeros((tq, tq), bool)
    ctx_ok = np.ones((l, tq), bool)
    variants = [
        np.concatenate([none, cur_ok, next_ok, ctx_ok], axis=0),
        np.concatenate([prev_ok, cur_ok, next_ok, ctx_ok], axis=0),
        np.concatenate([prev_ok, cur_ok, none, ctx_ok], axis=0),
        np.concatenate([none, none, none, ctx_ok], axis=0),
    ]
    return np.where(np.stack(variants), 0.0, NEG_INF).astype(np.float32)


NA_TILE_ROWS = 4
NA_HEADS_PER_STEP = 16


def _na_bias_t(rel_bias, l):
    col = np.arange(GRID_W)
    col_start = np.clip(col - NA_COLS // 2, 0, GRID_W - NA_COLS)
    col_valid = (col[None, :] >= col_start[:, None]) & (col[None, :] < col_start[:, None] + NA_COLS)
    dc_idx = np.clip(col[None, :] - col[:, None] + NA_COLS - 1, 0, 2 * NA_COLS - 2)
    bias_cols = (rel_bias.astype(F32) * LOG2E)[:, :, dc_idx]
    bias_cols = jnp.where(col_valid, bias_cols, NEG_INF)
    r = NA_TILE_ROWS
    a = np.arange(r)[:, None]
    c = np.arange(3 * r)[None, :]
    dr_idx = np.clip(c - r - a + NA_ROWS - 1, 0, 2 * NA_ROWS - 2)
    tiles = bias_cols[:, dr_idx]
    tiles = tiles.transpose(0, 2, 4, 1, 3)
    half = NA_ROWS // 2
    interior = (c - r >= a - half) & (c - r < a - half + NA_ROWS)
    first = (c >= r) & (c < r + NA_ROWS) & (a >= 0)
    last = (c >= 2 * r - NA_ROWS) & (c < 2 * r) & (a >= 0)
    none = np.zeros_like(interior)
    h = rel_bias.shape[0]
    out = []
    for ok in (first, interior, last, none):
        ok5 = ok.T[None, :, None, :, None]
        t = jnp.where(ok5, tiles, NEG_INF).reshape(h, 3 * r * GRID_W, r * GRID_W)
        out.append(jnp.concatenate([t, jnp.zeros((h, l, r * GRID_W), F32)], axis=1))
    return jnp.stack(out)


def _na_attn_kernel(qt_ref, kp, kc, kn, kx, vp, vc, vn, vx, bias_ref, ot_ref, s_a, s_b):
    jobs = []
    for hh in range(NA_HEADS_PER_STEP):
        rows = slice(hh * HEAD_V, (hh + 1) * HEAD_V)

        def write(o, rows=rows):
            ot_ref[0, rows, :] = o.astype(BF16)

        jobs.append(_BandJob(_slot_rhs(qt_ref[0, rows, :], hh % 2 == 1), hh // 2, hh * HEAD_V,
                             lambda off, nk, hh=hh: bias_ref[0, hh, off:off + nk, :], None, write))
    _run_band_jobs(jobs, [kp, kc, kn, kx], [vp, vc, vn, vx], (s_a, s_b))


def _na_attn(geom, qt, k, vt, bias, need_ctx):
    b, n, l, tpb, tm = geom.b, geom.n, geom.l, geom.tpb, geom.tm
    tiles_x = geom.xtpb
    nt = tiles_x + (1 if need_ctx else 0)
    hb = NA_HEADS_PER_STEP
    hw = hb * NA_HEAD_DIM
    width = 3 * tm + l

    def tile_of(bi, t):
        return bi * tpb + jnp.where(t < tiles_x, 1 + t, 0)

    def band_tile(bi, t, delta):
        return bi * tpb + 1 + jnp.clip(jnp.minimum(t, tiles_x - 1) + delta, 0, tiles_x - 1)

    q_spec = pl.BlockSpec((1, hw, tm), lambda g, t, bi: (tile_of(bi, t), g, 0))
    k_band = [pl.BlockSpec((tm, hw), lambda g, t, bi, dl=dl: (band_tile(bi, t, dl), g)) for dl in (-1, 0, 1)]
    v_band = [pl.BlockSpec((1, hw, tm), lambda g, t, bi, dl=dl: (band_tile(bi, t, dl), g, 0)) for dl in (-1, 0, 1)]
    in_specs = ([q_spec] + k_band + [pl.BlockSpec((l, hw), lambda g, t, bi: (bi * tpb, g))]
                + v_band + [pl.BlockSpec((1, hw, l), lambda g, t, bi: (bi * tpb, g, 0))]
                + [pl.BlockSpec((1, hb, width, tm), lambda g, t, bi: (_band_variant(t, tiles_x), g, 0, 0))])
    return pl.pallas_call(
        _na_attn_kernel,
        grid=(NA_HEADS // hb, nt, b),
        in_specs=in_specs,
        out_specs=q_spec,
        out_shape=jax.ShapeDtypeStruct((geom.tiles, NA_HEADS * NA_HEAD_DIM, tm), BF16),
        scratch_shapes=[pltpu.VMEM((width, tm), F32), pltpu.VMEM((width, tm), F32)],
        compiler_params=_cparams(3),
        name="na_attn",
    )(qt, k, k, k, k, vt, vt, vt, vt, bias)


def _post_kernel(*refs, final, ctx_period):
    x, refs = _resid_tile(refs, ctx_period)
    if final:
        o_ref, mod_ref, wo_ref, g_ref, w1_ref, w2_ref, gout_ref, out_ref = refs
    else:
        o_ref, mod_ref, wo_ref, g_ref, w1_ref, w2_ref, out_ref = refs
    d = D_MODEL
    mod = mod_ref[...]
    g1, sh2, sc2, g2 = mod[:, 2 * d:3 * d], mod[:, 3 * d:4 * d], mod[:, 4 * d:5 * d], mod[:, 5 * d:6 * d]
    y = lax.dot_general(o_ref[0], wo_ref[...], (((0,), (0,)), ((), ())), preferred_element_type=F32)
    x1 = x + g1 * y
    h2 = _norm_mod(x1, g_ref[...], sh2, sc2).astype(BF16)
    acc = jnp.zeros(x1.shape, F32)
    for c in range(D_FF // d):
        u = jnp.maximum(jnp.dot(h2, w1_ref[:, c * d:(c + 1) * d], preferred_element_type=F32), 0.0)
        acc = acc + jnp.dot((u * u).astype(BF16), w2_ref[c * d:(c + 1) * d, :], preferred_element_type=F32)
    x2 = x1 + g2 * acc
    if final:
        x2 = _rms(x2) * gout_ref[...]
    out_ref[...] = x2


def _post(geom, layer, resid, ot, ot_paired, mods, w_o, g, w1, w2, g_out):
    final = g_out is not None
    split = isinstance(resid, tuple)
    assert not (final and split)
    tm = geom.tm
    if final:
        tiles = geom.b * geom.xtpb
        tile_of = lambda t: (t // geom.xtpb) * geom.tpb + 1 + t % geom.xtpb
    else:
        tiles = geom.tiles
        tile_of = lambda t: t
    x_specs = _resid_specs(geom, resid) if split else [pl.BlockSpec((tm, D_MODEL), lambda t: (tile_of(t), 0))]
    if ot_paired:
        o_spec = pl.BlockSpec((1, D_MODEL, tm),
                              lambda t: (geom.pair_block(tile_of(t))[0], 0, geom.pair_block(tile_of(t))[1]))
    else:
        o_spec = pl.BlockSpec((1, D_MODEL, tm), lambda t: (tile_of(t), 0, 0))
    in_specs = x_specs + [o_spec, _mod_spec(geom, layer, tile_of),
                          _const_spec(w_o.shape), _const_spec((1, D_MODEL)), _const_spec(w1.shape),
                          _const_spec(w2.shape)]
    args = list(resid if split else (resid,)) + [ot, mods, w_o.astype(BF16), g.reshape(1, D_MODEL),
                                                  w1.astype(BF16), w2.astype(BF16)]
    if final:
        in_specs.append(_const_spec((1, D_MODEL)))
        args.append(g_out.reshape(1, D_MODEL))
    return pl.pallas_call(
        functools.partial(_post_kernel, final=final, ctx_period=geom.tpb if split else None),
        grid=(tiles,),
        in_specs=in_specs,
        out_specs=pl.BlockSpec((tm, D_MODEL), lambda t: (t, 0)),
        out_shape=jax.ShapeDtypeStruct((tiles * tm, D_MODEL), F32),
        compiler_params=_cparams(1),
        name="post_mlp",
    )(*args)


def kernel(x, c, ctx, c_ctx, ada_w, ada_b, norm_mix_g, norm_mlp_g, norm_out_g, mlp_w1, mlp_w2,
           mla_w_in, mla_qa_g, mla_w_qb, mla_kva_g, mla_w_kvb, mla_w_o,
           swa_w_qkv, swa_sink, swa_w_o, na_w_qkv, na_rel_bias, na_w_o):
    b, n, d = x.shape
    l = ctx.shape[1]
    assert d == D_MODEL and n % GRID_W == 0 and l % SWA_BLOCK == 0 and l == NA_TILE_ROWS * GRID_W
    geom = _Geom(b, n, l)

    mod_rows = -(-(b + 1) // 8) * 8
    c_rows = jnp.concatenate([c, c_ctx[None, :], jnp.zeros((mod_rows - b - 1, d), F32)], axis=0)
    mods = _ada_mods(c_rows, ada_w, ada_b).reshape(DEPTH, mod_rows, 1, N_MOD * d)

    xall = (x.reshape(b * n, d), ctx.reshape(b * l, d))
    for i in range(DEPTH):
        need_ctx = i < DEPTH - 1
        kind, j = i % 3, i // 3
        if kind == 0:
            qt, k, vt = _mla_pre(geom, i, xall, mods, norm_mix_g[i], mla_w_in[j], mla_qa_g[j], mla_w_qb[j],
                                 mla_kva_g[j], mla_w_kvb[j])
            o = _mla_attn(geom, qt, k, vt, need_ctx)
            w_o = mla_w_o[j]
        elif kind == 1:
            qt, k, vt = _swa_pre(geom, i, xall, mods, norm_mix_g[i], swa_w_qkv[j])
            o = _swa_attn(geom, qt, k, vt, swa_sink[j], need_ctx)
            w_o = swa_w_o[j]
        else:
            qt, k, vt = _na_pre(geom, i, xall, mods, norm_mix_g[i], na_w_qkv[j])
            o = _na_attn(geom, qt, k, vt, _na_bias_t(na_rel_bias[j], l), need_ctx)
            w_o = na_w_o[j]
        g_out = norm_out_g if i == DEPTH - 1 else None
        xall = _post(geom, i, xall, o, kind == 0, mods, w_o, norm_mlp_g[i], mlp_w1[i], mlp_w2[i], g_out)
    return xall.reshape(b, n, d)
```

```python
import functools
import math

import numpy as np
import jax
import jax.numpy as jnp
from jax import lax
from jax.experimental import pallas as pl
from jax.experimental.pallas import tpu as pltpu

F32 = jnp.float32
BF16 = jnp.bfloat16

D_MODEL = 1024
D_FF = 4 * D_MODEL
DEPTH = 4
N_MOD = 6
GRID_W = 64
EPS = 1e-6
ROPE_BASE = 10000.0
NEG_INF = -1e30

MLA_HEADS = 16
MLA_Q_RANK = 256
MLA_KV_RANK = 128
MLA_NOPE = 64
MLA_ROPE = 32
MLA_V = 64

SWA_Q_HEADS = 16
SWA_KV_HEADS = 4
SWA_HEAD_DIM = 64
SWA_WINDOW = 128
SWA_BLOCK = 128

NA_HEADS = 16
NA_HEAD_DIM = 64
NA_ROWS = 8
NA_COLS = 16

LANES = 128
HEAD_V = 64
CHUNK_UNROLL = 4
V_ROWS = 80
VMEM_LIMIT = 56 * 1024 * 1024
LOG2E = math.log2(math.e)


def _cparams(n_axes):
    return pltpu.CompilerParams(dimension_semantics=("arbitrary",) * n_axes,
                                vmem_limit_bytes=VMEM_LIMIT)


def _const_spec(shape):
    nd = len(shape)
    return pl.BlockSpec(shape, lambda *_: (0,) * nd, pipeline_mode=pl.Buffered(1))


def _rms(x):
    return x * lax.rsqrt(jnp.mean(x * x, axis=-1, keepdims=True) + EPS)


def _norm_mod(x, g, shift, scale):
    return (_rms(x) * g) * (1.0 + scale) + shift


def _ada_kernel(c_ref, w_ref, b_ref, o_ref):
    c = c_ref[...]
    act = c / (1.0 + jnp.exp(-c))
    o_ref[0] = jnp.dot(act.astype(BF16), w_ref[0], preferred_element_type=F32) + b_ref[0]


def _ada_mods(c_rows, ada_w, ada_b):
    depth, d, n = ada_w.shape
    tn = 1536
    return pl.pallas_call(
        _ada_kernel,
        grid=(depth, n // tn),
        in_specs=[
            pl.BlockSpec((c_rows.shape[0], d), lambda i, j: (0, 0)),
            pl.BlockSpec((1, d, tn), lambda i, j: (i, 0, j)),
            pl.BlockSpec((1, 1, tn), lambda i, j: (i, 0, j)),
        ],
        out_specs=pl.BlockSpec((1, c_rows.shape[0], tn), lambda i, j: (i, 0, j)),
        out_shape=jax.ShapeDtypeStruct((depth, c_rows.shape[0], n), F32),
        compiler_params=_cparams(2),
        name="ada_mods",
    )(c_rows, ada_w.astype(BF16), ada_b.reshape(depth, 1, n))


class _Geom:
    def __init__(self, b, n, l):
        assert n % l == 0
        self.b, self.n, self.l = b, n, l
        self.tm = l
        self.tpb = (n + l) // l
        self.xtpb = n // l
        self.tiles = b * self.tpb
        self.rows = self.tiles * self.tm
        self.qgroup = next(g for g in (8, 4, 2, 1) if self.xtpb % g == 0)
        self.qblocks = self.xtpb // self.qgroup + 1

    def mod_row(self, t):
        return jnp.where(t % self.tpb == 0, self.b, t // self.tpb)

    def pos_block(self, t):
        return t % self.tpb

    def pair_block(self, t):
        j = t % self.tpb
        block = (t // self.tpb) * self.qblocks + jnp.where(j == 0, self.qblocks - 1, (j - 1) // self.qgroup)
        return block, jnp.where(j == 0, 0, (j - 1) % self.qgroup)

    def out_block(self, t):
        j = jnp.maximum(t % self.tpb - 1, 0)
        return (t // self.tpb) * (self.qblocks - 1) + j // self.qgroup, j % self.qgroup


def _mod_spec(geom, layer, tile_of):
    return pl.BlockSpec((None, None, 1, N_MOD * D_MODEL),
                        lambda t: (layer, geom.mod_row(tile_of(t)), 0, 0))


def _row_spec(geom, width):
    return pl.BlockSpec((geom.tm, width), lambda t: (t, 0))


def _resid_specs(geom, resid):
    if not isinstance(resid, tuple):
        return [_row_spec(geom, D_MODEL)]

    def latent_map(t):
        return ((t // geom.tpb) * geom.xtpb + jnp.maximum(t % geom.tpb - 1, 0), 0)

    return [pl.BlockSpec((geom.tm, D_MODEL), latent_map),
            pl.BlockSpec((geom.tm, D_MODEL), lambda t: (t // geom.tpb, 0))]


def _resid_tile(refs, ctx_period):
    if ctx_period is None:
        return refs[0][...], refs[1:]
    is_ctx = pl.program_id(0) % ctx_period == 0
    return jnp.where(is_ctx, refs[1][...], refs[0][...]), refs[2:]


def _pos_spec(geom):
    return pl.BlockSpec((geom.tm, LANES), lambda t: (geom.pos_block(t), 0))


def _pos_spec_t(geom):
    return pl.BlockSpec((LANES, geom.tm), lambda t: (0, geom.pos_block(t)))


def _rope_cos_sin(n, rot_dim):
    t = jnp.arange(n, dtype=jnp.int32)
    row = (t // GRID_W).astype(F32)
    col = (t % GRID_W).astype(F32)
    n_freq = rot_dim // 4
    inv_freq = ROPE_BASE ** (-jnp.arange(n_freq, dtype=F32) / n_freq)
    ang = jnp.concatenate([row[:, None] * inv_freq, col[:, None] * inv_freq], axis=-1)
    return jnp.cos(ang), jnp.sin(ang)


def _rope_tables(geom, rot_dim, lead, scale):
    c, s = _rope_cos_sin(geom.n, rot_dim)
    half = rot_dim // 2
    pad = LANES - lead - rot_dim
    n, l = geom.n, geom.l
    cos_t = jnp.concatenate([jnp.ones((n, lead), F32), c, c, jnp.zeros((n, pad), F32)], axis=1)
    sin_t = jnp.concatenate([jnp.zeros((n, lead), F32), -s, s, jnp.zeros((n, pad), F32)], axis=1)
    cos_c = jnp.concatenate([jnp.ones((l, lead + 2 * half), F32), jnp.zeros((l, pad), F32)], axis=1)
    sin_c = jnp.zeros((l, LANES), F32)
    return (jnp.concatenate([cos_c, cos_t], axis=0) * scale,
            jnp.concatenate([sin_c, sin_t], axis=0) * scale)


def _ones_slot_vector(n_slots):
    row = np.zeros((n_slots, LANES), np.float32)
    row[:, HEAD_V] = 1.0
    return row.reshape(n_slots * LANES)


def _prep_mla_weights(w_in, w_qb, w_kvb):
    d = w_in.shape[0]
    pe0 = MLA_Q_RANK + MLA_KV_RANK
    ev = pe0 + 2 * np.arange(MLA_ROPE // 2)
    od = ev + 1
    z = jnp.zeros((d, LANES - MLA_ROPE), F32)
    win = jnp.concatenate([w_in[:, :pe0], w_in[:, ev], w_in[:, od], z, w_in[:, od], w_in[:, ev], z], axis=1)

    hd = MLA_NOPE + MLA_ROPE
    wq3 = w_qb.reshape(MLA_Q_RANK, MLA_HEADS, hd)
    nope, pe_e, pe_o = wq3[:, :, :MLA_NOPE], wq3[:, :, MLA_NOPE::2], wq3[:, :, MLA_NOPE + 1::2]
    zpad = jnp.zeros((MLA_Q_RANK, MLA_HEADS, LANES - hd), F32)
    zlead = jnp.zeros((MLA_Q_RANK, MLA_HEADS, MLA_NOPE), F32)
    wqa = jnp.concatenate([nope, pe_e, pe_o, zpad], axis=-1).reshape(MLA_Q_RANK, MLA_HEADS * LANES)
    wqb = jnp.concatenate([zlead, pe_o, pe_e, zpad], axis=-1).reshape(MLA_Q_RANK, MLA_HEADS * LANES)
    wq_t = jnp.concatenate([wqa, wqb], axis=1).T

    wkv3 = w_kvb.reshape(MLA_KV_RANK, MLA_HEADS, MLA_NOPE + MLA_V)
    zk = jnp.zeros((MLA_KV_RANK, MLA_HEADS, LANES - MLA_NOPE), F32)
    wk_top = jnp.concatenate([wkv3[:, :, :MLA_NOPE], zk], axis=-1).reshape(MLA_KV_RANK, MLA_HEADS * LANES)
    eye = np.zeros((LANES, MLA_HEADS, LANES), np.float32)
    for j in range(MLA_ROPE):
        eye[j, :, MLA_NOPE + j] = 1.0
    wk = jnp.concatenate([wk_top, jnp.asarray(eye.reshape(LANES, MLA_HEADS * LANES))], axis=0)
    zv = jnp.zeros((MLA_KV_RANK, MLA_HEADS, LANES - MLA_V), F32)
    wv_t = jnp.concatenate([wkv3[:, :, MLA_NOPE:], zv], axis=-1).reshape(MLA_KV_RANK, MLA_HEADS * LANES).T
    return win.astype(BF16), wq_t.astype(BF16), wk.astype(BF16), wv_t.astype(BF16)


def _mla_pre_kernel(*refs, ctx_period):
    x, refs = _resid_tile(refs, ctx_period)
    (mod_ref, g_ref, win_ref, qag_ref, kvag_ref, wqt_ref, wk_ref, wvt_ref,
     cqt_ref, sqt_ref, ck_ref, sk_ref, ones_ref, qt_out, k_out, vt_out) = refs
    nt = (((1,), (1,)), ((), ()))
    mod = mod_ref[...]
    h = _norm_mod(x, g_ref[...], mod[:, :D_MODEL], mod[:, D_MODEL:2 * D_MODEL]).astype(BF16)
    proj = jnp.dot(h, win_ref[...], preferred_element_type=F32)
    kv0 = MLA_Q_RANK
    pe0 = kv0 + MLA_KV_RANK
    qn = (_rms(proj[:, :kv0]) * qag_ref[...]).astype(BF16)
    kvn = _rms(proj[:, kv0:pe0]) * kvag_ref[...]
    qab_t = lax.dot_general(wqt_ref[...], qn, nt, preferred_element_type=F32)
    cqt, sqt = cqt_ref[...], sqt_ref[...]
    half = MLA_HEADS * LANES
    for hh in range(MLA_HEADS):
        lo = hh * LANES
        qt_out[0, lo:lo + LANES, :] = (qab_t[lo:lo + LANES] * cqt
                                       + qab_t[half + lo:half + lo + LANES] * sqt).astype(BF16)
    kpe = proj[:, pe0:pe0 + LANES] * ck_ref[...] + proj[:, pe0 + LANES:pe0 + 2 * LANES] * sk_ref[...]
    lat = jnp.concatenate([kvn, kpe], axis=1).astype(BF16)
    k_out[...] = jnp.dot(lat, wk_ref[...], preferred_element_type=F32).astype(BF16)
    vt = lax.dot_general(wvt_ref[...], kvn.astype(BF16), nt, preferred_element_type=F32)
    vt_out[0] = (vt + ones_ref[...]).astype(BF16)


def _mla_pre(geom, layer, resid, mods, g, w_in, qa_g, w_qb, kva_g, w_kvb):
    win, wq_t, wk, wv_t = _prep_mla_weights(w_in, w_qb, w_kvb)
    scale = (MLA_NOPE + MLA_ROPE) ** -0.5 * LOG2E
    cq, sq = _rope_tables(geom, MLA_ROPE, MLA_NOPE, scale)
    ck, sk = _rope_tables(geom, MLA_ROPE, 0, 1.0)
    width = MLA_HEADS * LANES
    tm = geom.tm
    t_spec = pl.BlockSpec((1, width, tm), lambda t: (t, 0, 0))
    t_shape = jax.ShapeDtypeStruct((geom.tiles, width, tm), BF16)
    q_spec = pl.BlockSpec((1, width, tm), lambda t: (geom.pair_block(t)[0], 0, geom.pair_block(t)[1]))
    q_shape = jax.ShapeDtypeStruct((geom.b * geom.qblocks, width, geom.qgroup * tm), BF16)
    split = isinstance(resid, tuple)
    return pl.pallas_call(
        functools.partial(_mla_pre_kernel, ctx_period=geom.tpb if split else None),
        grid=(geom.tiles,),
        in_specs=_resid_specs(geom, resid) + [
            _mod_spec(geom, layer, lambda t: t), _const_spec((1, D_MODEL)),
            _const_spec(win.shape), _const_spec((1, MLA_Q_RANK)), _const_spec((1, MLA_KV_RANK)),
            _const_spec(wq_t.shape), _const_spec(wk.shape), _const_spec(wv_t.shape),
            _pos_spec_t(geom), _pos_spec_t(geom), _pos_spec(geom), _pos_spec(geom),
            _const_spec((width, 1)),
        ],
        out_specs=[q_spec, _row_spec(geom, width), t_spec],
        out_shape=[q_shape, jax.ShapeDtypeStruct((geom.rows, width), BF16), t_shape],
        compiler_params=_cparams(1),
        name="mla_pre",
    )(*(resid if split else (resid,)), mods, g.reshape(1, D_MODEL), win, qa_g.reshape(1, -1),
      kva_g.reshape(1, -1), wq_t, wk, wv_t, cq.T, sq.T, ck, sk,
      jnp.asarray(_ones_slot_vector(MLA_HEADS).reshape(width, 1)))


def _normalised(accs):
    return jnp.concatenate([acc[:HEAD_V] * (1.0 / acc[HEAD_V:HEAD_V + 1]) for acc in accs], axis=0)


def _mla_attn_kernel(qt_ref, k_ref, vt_ref, ot_ref, *s_bufs, cs, n_chunks):
    tm = vt_ref.shape[2]
    tq = qt_ref.shape[2]
    n_tiles = tq // tm

    def qk(tile, c, par, j, u):
        r0 = pl.multiple_of((c * cs + u) * tm, tm)
        qt = qt_ref[0, j * LANES:(j + 1) * LANES, tile * tm:(tile + 1) * tm]
        s = jnp.dot(k_ref[pl.ds(r0, tm), j * LANES:(j + 1) * LANES], qt, preferred_element_type=F32)
        s_bufs[2 * tile + par][j, u * tm:(u + 1) * tm, :] = s
        return jnp.max(s.reshape(tm // 8, 8, tm), axis=0)

    def pv(tile, c, par, j, u, m_new):
        p = jnp.exp2(s_bufs[2 * tile + par][j, u * tm:(u + 1) * tm, :] - m_new).astype(BF16)
        return jnp.dot(vt_ref[c * cs + u, j * LANES:j * LANES + V_ROWS, :], p, preferred_element_type=F32)

    def first_scores():
        maxes = []
        for j in range(2):
            part = qk(0, 0, 0, j, 0)
            for u in range(1, cs):
                part = jnp.maximum(part, qk(0, 0, 0, j, u))
            maxes.append(jnp.max(part, axis=0, keepdims=True))
        return tuple(maxes)

    def step(tile, c, par, nxt, maxes, ms, accs):
        new_ms, new_accs, next_maxes = [], [], []
        for j in range(2):
            m_new = jnp.maximum(ms[j], maxes[j])
            alpha = jnp.exp2(ms[j] - m_new)
            part = qk(*nxt, j, 0) if nxt is not None else None
            acc_new = None
            for u in range(cs):
                if nxt is not None and u + 1 < cs:
                    part = jnp.maximum(part, qk(*nxt, j, u + 1))
                d = pv(tile, c, par, j, u, m_new)
                acc_new = d if acc_new is None else acc_new + d
            new_accs.append(alpha * accs[j] + acc_new)
            new_ms.append(m_new)
            if nxt is not None:
                next_maxes.append(jnp.max(part, axis=0, keepdims=True))
        return tuple(next_maxes), tuple(new_ms), tuple(new_accs)

    n_loop = (n_chunks - 1) // CHUNK_UNROLL
    maxes = first_scores()
    for tile in range(n_tiles):
        ms = tuple(jnp.full((1, tm), NEG_INF, F32) for _ in range(2))
        accs = tuple(jnp.zeros((V_ROWS, tm), F32) for _ in range(2))

        def body(i, carry, tile=tile):
            maxes, ms, accs = carry
            for r in range(CHUNK_UNROLL):
                c = CHUNK_UNROLL * i + r
                maxes, ms, accs = step(tile, c, r % 2, (tile, c + 1, (r + 1) % 2), maxes, ms, accs)
            return maxes, ms, accs

        maxes, ms, accs = lax.fori_loop(0, n_loop, body, (maxes, ms, accs))
        for c in range(n_loop * CHUNK_UNROLL, n_chunks):
            if c + 1 < n_chunks:
                nxt = (tile, c + 1, (c + 1) % 2)
            elif tile + 1 < n_tiles:
                nxt = (tile + 1, 0, 0)
            else:
                nxt = None
            maxes, ms, accs = step(tile, c, c % 2, nxt, maxes, ms, accs)
        ot_ref[0, :, tile * tm:(tile + 1) * tm] = _normalised(accs).astype(BF16)


def _mla_attn(geom, qt, k, vt):
    b, tpb, tm = geom.b, geom.tpb, geom.tm
    cs = 3 if tpb % 3 == 0 else (2 if tpb % 2 == 0 else 1)
    n_latent = geom.qblocks - 1
    tq = geom.qgroup * tm
    return pl.pallas_call(
        functools.partial(_mla_attn_kernel, cs=cs, n_chunks=tpb // cs),
        grid=(b, MLA_HEADS // 2, n_latent),
        in_specs=[pl.BlockSpec((1, 2 * LANES, tq), lambda bi, hp, t: (bi * geom.qblocks + t, hp, 0)),
                  pl.BlockSpec((tpb * tm, 2 * LANES), lambda bi, hp, t: (bi, hp)),
                  pl.BlockSpec((tpb, 2 * LANES, tm), lambda bi, hp, t: (bi, hp, 0))],
        out_specs=pl.BlockSpec((1, 2 * HEAD_V, tq), lambda bi, hp, t: (bi * n_latent + t, hp, 0)),
        out_shape=jax.ShapeDtypeStruct((b * n_latent, MLA_HEADS * HEAD_V, tq), BF16),
        scratch_shapes=[pltpu.VMEM((2, cs * tm, tm), F32)] * (2 * (tq // tm)),
        compiler_params=_cparams(3),
        name="mla_attn",
    )(qt, k, vt)


def _mla_ctx_attn_kernel(qt_ref, k_ref, vt_ref, ot_ref):
    accs = []
    for j in range(2):
        s = jnp.dot(k_ref[:, j * LANES:(j + 1) * LANES], qt_ref[0, j * LANES:(j + 1) * LANES, :],
                    preferred_element_type=F32)
        p = jnp.exp2(s - jnp.max(s, axis=0, keepdims=True)).astype(BF16)
        accs.append(jnp.dot(vt_ref[0, j * LANES:j * LANES + V_ROWS, :], p, preferred_element_type=F32))
    ot_ref[0] = _normalised(accs).astype(BF16)


def _mla_ctx_attn(geom, qt, k, vt):
    b, tpb, tm = geom.b, geom.tpb, geom.tm
    return pl.pallas_call(
        _mla_ctx_attn_kernel,
        grid=(b, MLA_HEADS // 2),
        in_specs=[pl.BlockSpec((1, 2 * LANES, tm), lambda bi, hp: (bi * geom.qblocks + geom.qblocks - 1, hp, 0)),
                  pl.BlockSpec((tm, 2 * LANES), lambda bi, hp: (bi * tpb, hp)),
                  pl.BlockSpec((1, 2 * LANES, tm), lambda bi, hp: (bi * tpb, hp, 0))],
        out_specs=pl.BlockSpec((1, 2 * HEAD_V, tm), lambda bi, hp: (bi, hp, 0)),
        out_shape=jax.ShapeDtypeStruct((b, MLA_HEADS * HEAD_V, tm), BF16),
        compiler_params=_cparams(2),
        name="mla_ctx_attn",
    )(qt, k, vt)


def _head_rope_tables(geom, scale):
    c, s = _rope_cos_sin(geom.n, SWA_HEAD_DIM)
    cos_t = jnp.concatenate([jnp.ones((geom.l, SWA_HEAD_DIM), F32), jnp.concatenate([c, c], axis=1)], axis=0)
    sin_t = jnp.concatenate([jnp.zeros((geom.l, SWA_HEAD_DIM), F32), jnp.concatenate([-s, s], axis=1)], axis=0)
    return cos_t * scale, sin_t * scale


def _prep_swa_weights(w_qkv):
    d = w_qkv.shape[0]
    dq = SWA_Q_HEADS * SWA_HEAD_DIM
    dkv = SWA_KV_HEADS * SWA_HEAD_DIM

    def rope_pair(w, heads):
        w3 = w.reshape(d, heads, SWA_HEAD_DIM)
        e, o = w3[:, :, 0::2], w3[:, :, 1::2]
        return (jnp.concatenate([e, o], axis=-1).reshape(d, heads * SWA_HEAD_DIM),
                jnp.concatenate([o, e], axis=-1).reshape(d, heads * SWA_HEAD_DIM))

    wqa, wqb = rope_pair(w_qkv[:, :dq], SWA_Q_HEADS)
    wka, wkb = rope_pair(w_qkv[:, dq:dq + dkv], SWA_KV_HEADS)
    w_t = jnp.concatenate([wqa, wqb, w_qkv[:, dq + dkv:]], axis=1).T
    return w_t.astype(BF16), jnp.concatenate([wka, wkb], axis=1).astype(BF16)


def _swa_pre_kernel(x_ref, mod_ref, g_ref, wt_ref, wk_ref, cqt_ref, sqt_ref, ck_ref, sk_ref,
                    qt_out, k_out, vt_out):
    nt = (((1,), (1,)), ((), ()))
    mod = mod_ref[...]
    h = _norm_mod(x_ref[...], g_ref[...], mod[:, :D_MODEL], mod[:, D_MODEL:2 * D_MODEL]).astype(BF16)
    tm = h.shape[0]
    dq = SWA_Q_HEADS * SWA_HEAD_DIM
    dkv = SWA_KV_HEADS * SWA_HEAD_DIM
    r = lax.dot_general(wt_ref[...], h, nt, preferred_element_type=F32)
    qa = r[:dq].reshape(SWA_Q_HEADS, SWA_HEAD_DIM, tm)
    qb = r[dq:2 * dq].reshape(SWA_Q_HEADS, SWA_HEAD_DIM, tm)
    qt_out[0] = (qa * cqt_ref[...][None] + qb * sqt_ref[...][None]).reshape(dq, tm).astype(BF16)
    vt_out[0] = r[2 * dq:].astype(BF16)
    kab = jnp.dot(h, wk_ref[...], preferred_element_type=F32)
    k_out[...] = (kab[:, :dkv] * ck_ref[...] + kab[:, dkv:] * sk_ref[...]).astype(BF16)


def _swa_pre(geom, layer, xall, mods, g, w_qkv):
    w_t, wk = _prep_swa_weights(w_qkv)
    cq, sq = _head_rope_tables(geom, SWA_HEAD_DIM ** -0.5 * LOG2E)
    ck, sk = _head_rope_tables(geom, 1.0)
    ck, sk = jnp.tile(ck, (1, SWA_KV_HEADS)), jnp.tile(sk, (1, SWA_KV_HEADS))
    tm = geom.tm
    dq, dkv = SWA_Q_HEADS * SWA_HEAD_DIM, SWA_KV_HEADS * SWA_HEAD_DIM
    qt_pos = pl.BlockSpec((SWA_HEAD_DIM, tm), lambda t: (0, geom.pos_block(t)))
    k_pos = pl.BlockSpec((tm, dkv), lambda t: (geom.pos_block(t), 0))
    return pl.pallas_call(
        _swa_pre_kernel,
        grid=(geom.tiles,),
        in_specs=[
            _row_spec(geom, D_MODEL), _mod_spec(geom, layer, lambda t: t), _const_spec((1, D_MODEL)),
            _const_spec(w_t.shape), _const_spec(wk.shape), qt_pos, qt_pos, k_pos, k_pos,
        ],
        out_specs=[pl.BlockSpec((1, dq, tm), lambda t: (t, 0, 0)), _row_spec(geom, dkv),
                   pl.BlockSpec((1, dkv, tm), lambda t: (t, 0, 0))],
        out_shape=[jax.ShapeDtypeStruct((geom.tiles, dq, tm), BF16),
                   jax.ShapeDtypeStruct((geom.rows, dkv), BF16),
                   jax.ShapeDtypeStruct((geom.tiles, dkv, tm), BF16)],
        compiler_params=_cparams(1),
        name="swa_pre",
    )(xall, mods, g.reshape(1, D_MODEL), w_t, wk, cq.T, sq.T, ck, sk)


def _na_pre_kernel(x_ref, mod_ref, g_ref, wt_ref, wk_ref, qt_out, k_out, vt_out):
    nt = (((1,), (1,)), ((), ()))
    mod = mod_ref[...]
    h = _norm_mod(x_ref[...], g_ref[...], mod[:, :D_MODEL], mod[:, D_MODEL:2 * D_MODEL]).astype(BF16)
    dm = NA_HEADS * NA_HEAD_DIM
    r = lax.dot_general(wt_ref[...], h, nt, preferred_element_type=F32)
    qt_out[0] = r[:dm].astype(BF16)
    vt_out[0] = r[dm:].astype(BF16)
    k_out[...] = jnp.dot(h, wk_ref[...], preferred_element_type=F32).astype(BF16)


def _na_pre(geom, layer, xall, mods, g, w_qkv):
    dm = NA_HEADS * NA_HEAD_DIM
    scale = NA_HEAD_DIM ** -0.5 * LOG2E
    w_t = jnp.concatenate([w_qkv[:, :dm] * scale, w_qkv[:, 2 * dm:]], axis=1).T.astype(BF16)
    wk = w_qkv[:, dm:2 * dm].astype(BF16)
    tm = geom.tm
    t_spec = pl.BlockSpec((1, dm, tm), lambda t: (t, 0, 0))
    t_shape = jax.ShapeDtypeStruct((geom.tiles, dm, tm), BF16)
    return pl.pallas_call(
        _na_pre_kernel,
        grid=(geom.tiles,),
        in_specs=[_row_spec(geom, D_MODEL), _mod_spec(geom, layer, lambda t: t), _const_spec((1, D_MODEL)),
                  _const_spec(w_t.shape), _const_spec(wk.shape)],
        out_specs=[t_spec, _row_spec(geom, dm), t_spec],
        out_shape=[t_shape, jax.ShapeDtypeStruct((geom.rows, dm), BF16), t_shape],
        compiler_params=_cparams(1),
        name="na_pre",
    )(xall, mods, g.reshape(1, D_MODEL), w_t, wk)


class _BandJob:
    def __init__(self, rhs, slot, vrow, bias, sink, write):
        self.rhs, self.slot, self.vrow, self.bias, self.sink, self.write = rhs, slot, vrow, bias, sink, write


def _slot_rhs(qt, upper):
    zeros = jnp.zeros_like(qt)
    return jnp.concatenate([zeros, qt] if upper else [qt, zeros], axis=0)


def _run_band_jobs(jobs, k_refs, vt_refs, s_bufs):
    sizes = [r.shape[0] for r in k_refs]
    offs = [sum(sizes[:i]) for i in range(len(sizes))]
    nblk = len(sizes)

    def qk(job, u, buf):
        nk, off = sizes[u], offs[u]
        s = jnp.dot(k_refs[u][:, job.slot * LANES:(job.slot + 1) * LANES], job.rhs,
                    preferred_element_type=F32) + job.bias(off, nk)
        buf[off:off + nk, :] = s
        return jnp.max(s.reshape(nk // 8, 8, s.shape[1]), axis=0)

    def pv(job, u, buf, m):
        nk, off = sizes[u], offs[u]
        p = jnp.exp2(buf[off:off + nk, :] - m).astype(BF16)
        lhs = jnp.concatenate([vt_refs[u][0, job.vrow:job.vrow + HEAD_V, :],
                               jnp.ones((V_ROWS - HEAD_V, nk), BF16)], axis=0)
        return jnp.dot(lhs, p, preferred_element_type=F32)

    part = qk(jobs[0], 0, s_bufs[0])
    for u in range(1, nblk):
        part = jnp.maximum(part, qk(jobs[0], u, s_bufs[0]))
    for i, job in enumerate(jobs):
        nxt = jobs[i + 1] if i + 1 < len(jobs) else None
        cur_buf, nxt_buf = s_bufs[i % 2], s_bufs[(i + 1) % 2]
        m = jnp.max(part, axis=0, keepdims=True)
        if job.sink is not None:
            m = jnp.maximum(m, job.sink)
        part = qk(nxt, 0, nxt_buf) if nxt is not None else None
        acc = None
        for u in range(nblk):
            if nxt is not None and u + 1 < nblk:
                part = jnp.maximum(part, qk(nxt, u + 1, nxt_buf))
            d = pv(job, u, cur_buf, m)
            acc = d if acc is None else acc + d
        denom = acc[HEAD_V:HEAD_V + 1]
        if job.sink is not None:
            denom = denom + jnp.exp2(job.sink - m)
        job.write(acc[:HEAD_V] * (1.0 / denom))


def _band_variant(t, tiles_x):
    return jnp.where(t >= tiles_x, 3, jnp.where(t == 0, 0, jnp.where(t == tiles_x - 1, 2, 1)))


def _swa_attn_kernel(qt_ref, kp, kc, kn, kx, vp, vc, vn, vx, mask_ref, sink_ref, ot_ref, s_a, s_b):
    tq = qt_ref.shape[2]
    grp = SWA_Q_HEADS // SWA_KV_HEADS
    lane = lax.broadcasted_iota(jnp.int32, (1, 2 * tq), 1)
    jobs = []
    for g in range(SWA_KV_HEADS):
        for i in range(grp // 2):
            h0 = g * grp + 2 * i
            qt = jnp.concatenate([qt_ref[0, h0 * HEAD_V:(h0 + 1) * HEAD_V, :],
                                  qt_ref[0, (h0 + 1) * HEAD_V:(h0 + 2) * HEAD_V, :]], axis=1)
            sink = jnp.where(lane < tq, sink_ref[h0], sink_ref[h0 + 1]) * LOG2E

            def write(o, h0=h0):
                ot_ref[0, h0 * HEAD_V:(h0 + 1) * HEAD_V, :] = o[:, :tq].astype(BF16)
                ot_ref[0, (h0 + 1) * HEAD_V:(h0 + 2) * HEAD_V, :] = o[:, tq:].astype(BF16)

            jobs.append(_BandJob(_slot_rhs(qt, g % 2 == 1), g // 2, g * HEAD_V,
                                 lambda off, nk: mask_ref[0, off:off + nk, :], sink, write))
    _run_band_jobs(jobs, [kp, kc, kn, kx], [vp, vc, vn, vx], (s_a, s_b))


def _swa_attn(geom, qt, k, vt, sink, need_ctx):
    b, n, l, tpb, tm = geom.b, geom.n, geom.l, geom.tpb, geom.tm
    tq = SWA_BLOCK
    per = tm // tq
    tiles_x = n // tq
    nt = tiles_x + (l // tq if need_ctx else 0)
    dq, dkv = SWA_Q_HEADS * SWA_HEAD_DIM, SWA_KV_HEADS * SWA_HEAD_DIM
    width = 3 * tq + l
    mask = jnp.asarray(np.tile(_swa_mask_t(l), (1, 1, 2)))

    def q_map(bi, t):
        return (bi * tpb + jnp.where(t < tiles_x, 1 + t // per, 0), 0,
                jnp.where(t < tiles_x, t % per, t - tiles_x))

    def band_tile(t, delta):
        return jnp.clip(jnp.minimum(t, tiles_x - 1) + delta, 0, tiles_x - 1)

    k_band = [pl.BlockSpec((tq, dkv), lambda bi, t, dl=dl: (bi * tpb * per + per + band_tile(t, dl), 0))
              for dl in (-1, 0, 1)]
    v_band = [pl.BlockSpec((1, dkv, tq), lambda bi, t, dl=dl: (bi * tpb + 1 + band_tile(t, dl) // per, 0,
                                                                band_tile(t, dl) % per))
              for dl in (-1, 0, 1)]
    in_specs = ([pl.BlockSpec((1, dq, tq), q_map)]
                + k_band + [pl.BlockSpec((l, dkv), lambda bi, t: (bi * tpb, 0))]
                + v_band + [pl.BlockSpec((1, dkv, l), lambda bi, t: (bi * tpb, 0, 0))]
                + [pl.BlockSpec((1, width, 2 * tq), lambda bi, t: (_band_variant(t, tiles_x), 0, 0)),
                   pl.BlockSpec(memory_space=pltpu.SMEM)])
    return pl.pallas_call(
        _swa_attn_kernel,
        grid=(b, nt),
        in_specs=in_specs,
        out_specs=pl.BlockSpec((1, dq, tq), q_map),
        out_shape=jax.ShapeDtypeStruct((geom.tiles, dq, tm), BF16),
        scratch_shapes=[pltpu.VMEM((width, 2 * tq), F32), pltpu.VMEM((width, 2 * tq), F32)],
        compiler_params=_cparams(2),
        name="swa_attn",
    )(qt, k, k, k, k, vt, vt, vt, vt, mask, sink)


def _swa_mask_t(l):
    tq = SWA_BLOCK
    j = np.arange(tq)[:, None]
    i = np.arange(tq)[None, :]
    prev_ok = (SWA_BLOCK + i - j) <= SWA_WINDOW
    next_ok = (SWA_BLOCK + j - i) <= SWA_WINDOW
    cur_ok = np.abs(i - j) <= SWA_WINDOW
    none = np.zeros((tq, tq), bool)
    ctx_ok = np.ones((l, tq), bool)
    variants = [
        np.concatenate([none, cur_ok, next_ok, ctx_ok], axis=0),
        np.concatenate([prev_ok, cur_ok, next_ok, ctx_ok], axis=0),
        np.concatenate([prev_ok, cur_ok, none, ctx_ok], axis=0),
        np.concatenate([none, none, none, ctx_ok], axis=0),
    ]
    return np.where(np.stack(variants), 0.0, NEG_INF).astype(np.float32)


NA_TILE_ROWS = 4
NA_HEADS_PER_STEP = 16


def _na_bias_t(rel_bias, l):
    col = np.arange(GRID_W)
    col_start = np.clip(col - NA_COLS // 2, 0, GRID_W - NA_COLS)
    col_valid = (col[None, :] >= col_start[:, None]) & (col[None, :] < col_start[:, None] + NA_COLS)
    dc_idx = np.clip(col[None, :] - col[:, None] + NA_COLS - 1, 0, 2 * NA_COLS - 2)
    bias_cols = (rel_bias.astype(F32) * LOG2E)[:, :, dc_idx]
    bias_cols = jnp.where(col_valid, bias_cols, NEG_INF)
    r = NA_TILE_ROWS
    a = np.arange(r)[:, None]
    c = np.arange(3 * r)[None, :]
    dr_idx = np.clip(c - r - a + NA_ROWS - 1, 0, 2 * NA_ROWS - 2)
    tiles = bias_cols[:, dr_idx]
    tiles = tiles.transpose(0, 2, 4, 1, 3)
    half = NA_ROWS // 2
    interior = (c - r >= a - half) & (c - r < a - half + NA_ROWS)
    first = (c >= r) & (c < r + NA_ROWS) & (a >= 0)
    last = (c >= 2 * r - NA_ROWS) & (c < 2 * r) & (a >= 0)
    none = np.zeros_like(interior)
    h = rel_bias.shape[0]
    out = []
    for ok in (first, interior, last, none):
        ok5 = ok.T[None, :, None, :, None]
        t = jnp.where(ok5, tiles, NEG_INF).reshape(h, 3 * r * GRID_W, r * GRID_W)
        out.append(jnp.concatenate([t, jnp.zeros((h, l, r * GRID_W), F32)], axis=1))
    return jnp.stack(out)


def _na_attn_kernel(qt_ref, kp, kc, kn, kx, vp, vc, vn, vx, bias_ref, ot_ref, s_a, s_b):
    jobs = []
    for hh in range(NA_HEADS_PER_STEP):
        rows = slice(hh * HEAD_V, (hh + 1) * HEAD_V)

        def write(o, rows=rows):
            ot_ref[0, rows, :] = o.astype(BF16)

        jobs.append(_BandJob(_slot_rhs(qt_ref[0, rows, :], hh % 2 == 1), hh // 2, hh * HEAD_V,
                             lambda off, nk, hh=hh: bias_ref[0, hh, off:off + nk, :], None, write))
    _run_band_jobs(jobs, [kp, kc, kn, kx], [vp, vc, vn, vx], (s_a, s_b))


def _na_attn(geom, qt, k, vt, bias, need_ctx):
    b, n, l, tpb, tm = geom.b, geom.n, geom.l, geom.tpb, geom.tm
    tiles_x = geom.xtpb
    nt = tiles_x + (1 if need_ctx else 0)
    hb = NA_HEADS_PER_STEP
    hw = hb * NA_HEAD_DIM
    width = 3 * tm + l

    def tile_of(bi, t):
        return bi * tpb + jnp.where(t < tiles_x, 1 + t, 0)

    def band_tile(bi, t, delta):
        return bi * tpb + 1 + jnp.clip(jnp.minimum(t, tiles_x - 1) + delta, 0, tiles_x - 1)

    q_spec = pl.BlockSpec((1, hw, tm), lambda g, t, bi: (tile_of(bi, t), g, 0))
    k_band = [pl.BlockSpec((tm, hw), lambda g, t, bi, dl=dl: (band_tile(bi, t, dl), g)) for dl in (-1, 0, 1)]
    v_band = [pl.BlockSpec((1, hw, tm), lambda g, t, bi, dl=dl: (band_tile(bi, t, dl), g, 0)) for dl in (-1, 0, 1)]
    in_specs = ([q_spec] + k_band + [pl.BlockSpec((l, hw), lambda g, t, bi: (bi * tpb, g))]
                + v_band + [pl.BlockSpec((1, hw, l), lambda g, t, bi: (bi * tpb, g, 0))]
                + [pl.BlockSpec((1, hb, width, tm), lambda g, t, bi: (_band_variant(t, tiles_x), g, 0, 0))])
    return pl.pallas_call(
        _na_attn_kernel,
        grid=(NA_HEADS // hb, nt, b),
        in_specs=in_specs,
        out_specs=q_spec,
        out_shape=jax.ShapeDtypeStruct((geom.tiles, NA_HEADS * NA_HEAD_DIM, tm), BF16),
        scratch_shapes=[pltpu.VMEM((width, tm), F32), pltpu.VMEM((width, tm), F32)],
        compiler_params=_cparams(3),
        name="na_attn",
    )(qt, k, k, k, k, vt, vt, vt, vt, bias)


def _post_kernel(*refs, final, ctx_period, octx_period):
    x, refs = _resid_tile(refs, ctx_period)
    ot = refs[0][0]
    if octx_period is not None:
        ot = jnp.where(pl.program_id(0) % octx_period == 0, refs[1][0], ot)
        refs = refs[1:]
    if final:
        _, mod_ref, wo_ref, g_ref, w1_ref, w2_ref, gout_ref, out_ref = refs
    else:
        _, mod_ref, wo_ref, g_ref, w1_ref, w2_ref, out_ref = refs
    d = D_MODEL
    mod = mod_ref[...]
    g1, sh2, sc2, g2 = mod[:, 2 * d:3 * d], mod[:, 3 * d:4 * d], mod[:, 4 * d:5 * d], mod[:, 5 * d:6 * d]
    y = lax.dot_general(ot, wo_ref[...], (((0,), (0,)), ((), ())), preferred_element_type=F32)
    x1 = x + g1 * y
    h2 = _norm_mod(x1, g_ref[...], sh2, sc2).astype(BF16)
    acc = jnp.zeros(x1.shape, F32)
    for c in range(D_FF // d):
        u = jnp.maximum(jnp.dot(h2, w1_ref[:, c * d:(c + 1) * d], preferred_element_type=F32), 0.0)
        acc = acc + jnp.dot((u * u).astype(BF16), w2_ref[c * d:(c + 1) * d, :], preferred_element_type=F32)
    x2 = x1 + g2 * acc
    if final:
        x2 = _rms(x2) * gout_ref[...]
    out_ref[...] = x2


def _post(geom, layer, resid, ot, octx, ot_grouped, mods, w_o, g, w1, w2, g_out):
    final = g_out is not None
    split = isinstance(resid, tuple)
    assert not (final and split)
    tm = geom.tm
    if final:
        tiles = geom.b * geom.xtpb
        tile_of = lambda t: (t // geom.xtpb) * geom.tpb + 1 + t % geom.xtpb
    else:
        tiles = geom.tiles
        tile_of = lambda t: t
    x_specs = _resid_specs(geom, resid) if split else [pl.BlockSpec((tm, D_MODEL), lambda t: (tile_of(t), 0))]
    if ot_grouped:
        o_specs = [pl.BlockSpec((1, D_MODEL, tm),
                                lambda t: (geom.out_block(tile_of(t))[0], 0, geom.out_block(tile_of(t))[1]))]
    else:
        o_specs = [pl.BlockSpec((1, D_MODEL, tm), lambda t: (tile_of(t), 0, 0))]
    o_args = [ot]
    if octx is not None:
        assert not final
        o_specs.append(pl.BlockSpec((1, D_MODEL, tm), lambda t: (t // geom.tpb, 0, 0)))
        o_args.append(octx)
    in_specs = x_specs + o_specs + [_mod_spec(geom, layer, tile_of),
                          _const_spec(w_o.shape), _const_spec((1, D_MODEL)), _const_spec(w1.shape),
                          _const_spec(w2.shape)]
    args = list(resid if split else (resid,)) + o_args + [mods, w_o.astype(BF16), g.reshape(1, D_MODEL),
                                                          w1.astype(BF16), w2.astype(BF16)]
    if final:
        in_specs.append(_const_spec((1, D_MODEL)))
        args.append(g_out.reshape(1, D_MODEL))
    return pl.pallas_call(
        functools.partial(_post_kernel, final=final, ctx_period=geom.tpb if split else None,
                          octx_period=geom.tpb if octx is not None else None),
        grid=(tiles,),
        in_specs=in_specs,
        out_specs=pl.BlockSpec((tm, D_MODEL), lambda t: (t, 0)),
        out_shape=jax.ShapeDtypeStruct((tiles * tm, D_MODEL), F32),
        compiler_params=_cparams(1),
        name="post_mlp",
    )(*args)


def kernel(x, c, ctx, c_ctx, ada_w, ada_b, norm_mix_g, norm_mlp_g, norm_out_g, mlp_w1, mlp_w2,
           mla_w_in, mla_qa_g, mla_w_qb, mla_kva_g, mla_w_kvb, mla_w_o,
           swa_w_qkv, swa_sink, swa_w_o, na_w_qkv, na_rel_bias, na_w_o):
    b, n, d = x.shape
    l = ctx.shape[1]
    assert d == D_MODEL and n % GRID_W == 0 and l % SWA_BLOCK == 0 and l == NA_TILE_ROWS * GRID_W
    geom = _Geom(b, n, l)

    mod_rows = -(-(b + 1) // 8) * 8
    c_rows = jnp.concatenate([c, c_ctx[None, :], jnp.zeros((mod_rows - b - 1, d), F32)], axis=0)
    mods = _ada_mods(c_rows, ada_w, ada_b).reshape(DEPTH, mod_rows, 1, N_MOD * d)

    xall = (x.reshape(b * n, d), ctx.reshape(b * l, d))
    for i in range(DEPTH):
        need_ctx = i < DEPTH - 1
        kind, j = i % 3, i // 3
        if kind == 0:
            qt, k, vt = _mla_pre(geom, i, xall, mods, norm_mix_g[i], mla_w_in[j], mla_qa_g[j], mla_w_qb[j],
                                 mla_kva_g[j], mla_w_kvb[j])
            o = _mla_attn(geom, qt, k, vt)
            octx = _mla_ctx_attn(geom, qt, k, vt) if need_ctx else None
            w_o = mla_w_o[j]
        elif kind == 1:
            qt, k, vt = _swa_pre(geom, i, xall, mods, norm_mix_g[i], swa_w_qkv[j])
            o, octx = _swa_attn(geom, qt, k, vt, swa_sink[j], need_ctx), None
            w_o = swa_w_o[j]
        else:
            qt, k, vt = _na_pre(geom, i, xall, mods, norm_mix_g[i], na_w_qkv[j])
            o, octx = _na_attn(geom, qt, k, vt, _na_bias_t(na_rel_bias[j], l), need_ctx), None
            w_o = na_w_o[j]
        g_out = norm_out_g if i == DEPTH - 1 else None
        xall = _post(geom, i, xall, o, octx, kind == 0, mods, w_o, norm_mlp_g[i], mlp_w1[i], mlp_w2[i], g_out)
    return xall.reshape(b, n, d)
```
